```python
import jax, jax.numpy as jnp
from jax import lax
import numpy as np

D_MODEL = 2048
BATCH = 4
SEQ = 2048
DEPTH = 4
DEC_BATCH = 8
DEC_SEQ = 1
PAST_LEN = 16384
PAGE_SIZE = 128

HEAD_DIM = 128
H_SB = D_MODEL // (4 * HEAD_DIM)
H_GLA = D_MODEL // (2 * HEAD_DIM)
H_FOX = D_MODEL // (4 * HEAD_DIM)
DK_GLA = HEAD_DIM // 2
DV_GLA = HEAD_DIM
W_SB = H_SB * HEAD_DIM
W_GLA = H_GLA * DV_GLA
W_FOX = H_FOX * HEAD_DIM
MIX_WIDTH = W_SB + W_GLA + W_FOX
GATE_RANK = 16
GATE_TAU = 16.0
GLA_CHUNK = 64
Q_BLOCK = 128
RMS_EPS = 1e-6
SPLIT_SIZES = (W_SB, W_SB, W_SB, W_SB,
               H_GLA * DK_GLA, H_GLA * DK_GLA, W_GLA, W_GLA, GATE_RANK,
               W_FOX, W_FOX, W_FOX, W_FOX, H_FOX)
SPLIT_IDX = tuple(int(i) for i in np.cumsum(SPLIT_SIZES)[:-1])
N_IN = int(sum(SPLIT_SIZES))

kernel_name = "hybrid_sb_gla_fox_step"


def rms_norm(x, g):
    xf = x.astype(jnp.float32)
    y = xf * lax.rsqrt(jnp.mean(xf * xf, axis=-1, keepdims=True) + RMS_EPS)
    return (y * g.astype(jnp.float32)).astype(x.dtype)


def project(x, norm_g, w_in, w_gate_b, b_gate, b_forget, q_norm_g, k_norm_g):
    B, T, _ = x.shape
    h = rms_norm(x, norm_g)
    p = h @ w_in
    qa, ka, va, ga, qb, kb, vb, gb, ab, qc, kc, vc, gc, fc = jnp.split(p, SPLIT_IDX, axis=-1)
    heads = lambda t, n, d: t.reshape(B, T, n, d)
    log_alpha = jax.nn.log_sigmoid((ab @ w_gate_b + b_gate).astype(jnp.float32)) / GATE_TAU
    log_f = jax.nn.log_sigmoid((fc + b_forget).astype(jnp.float32))
    qc = rms_norm(heads(qc, H_FOX, HEAD_DIM), q_norm_g)
    kc = rms_norm(heads(kc, H_FOX, HEAD_DIM), k_norm_g)
    return (heads(qa, H_SB, HEAD_DIM), heads(ka, H_SB, HEAD_DIM), heads(va, H_SB, HEAD_DIM), ga,
            heads(qb, H_GLA, DK_GLA), heads(kb, H_GLA, DK_GLA), heads(vb, H_GLA, DV_GLA), gb,
            heads(log_alpha, H_GLA, DK_GLA),
            qc, kc, heads(vc, H_FOX, HEAD_DIM), gc, log_f)


def sb_combine(q, k, v, mask):
    z = jnp.einsum('bqhd,bkhd->bhqk', q, k).astype(jnp.float32) * (q.shape[-1] ** -0.5)
    log_beta = jax.nn.log_sigmoid(z)
    log_1m = jnp.where(mask, jax.nn.log_sigmoid(-z), 0.0)
    acc = lax.cumsum(log_1m, axis=3, reverse=True) - log_1m
    w = jnp.where(mask, jnp.exp(log_beta + acc), 0.0)
    return jnp.einsum('bhqk,bkhd->bqhd', w.astype(v.dtype), v)


def sb_prompt(q, k, v):
    B, T, H, d = q.shape
    nb = T // Q_BLOCK
    qb = jnp.moveaxis(q.reshape(B, nb, Q_BLOCK, H, d), 1, 0)
    key_pos = jnp.arange(T)

    def block(args):
        qi, i = args
        q_pos = i * Q_BLOCK + jnp.arange(Q_BLOCK)
        return sb_combine(qi, k, v, key_pos[None, :] < q_pos[:, None])

    out = lax.map(block, (qb, jnp.arange(nb)))
    return jnp.moveaxis(out, 0, 1).reshape(B, T, H, d)


def sb_decode(q, k_all, v_all, past_len):
    Td, L = q.shape[1], k_all.shape[1]
    mask = jnp.arange(L)[None, :] < (past_len + jnp.arange(Td))[:, None]
    return sb_combine(q, k_all, v_all, mask)


def fox_combine(q, k, v, Fq, Fk, mask):
    z = jnp.einsum('bqhd,bkhd->bhqk', q, k).astype(jnp.float32) * (q.shape[-1] ** -0.5)
    z = z + (Fq[..., :, None] - Fk[..., None, :])
    p = jax.nn.softmax(jnp.where(mask, z, -jnp.inf), axis=-1)
    return jnp.einsum('bhqk,bkhd->bqhd', p.astype(v.dtype), v)


def fox_prompt(q, k, v, log_f):
    B, T, H, d = q.shape
    nb = T // Q_BLOCK
    F = jnp.cumsum(log_f, axis=1)
    qb = jnp.moveaxis(q.reshape(B, nb, Q_BLOCK, H, d), 1, 0)
    Fb = jnp.moveaxis(F.reshape(B, nb, Q_BLOCK, H), 1, 0)
    Fk = jnp.swapaxes(F, 1, 2)
    key_pos = jnp.arange(T)

    def block(args):
        qi, Fi, i = args
        q_pos = i * Q_BLOCK + jnp.arange(Q_BLOCK)
        return fox_combine(qi, k, v, jnp.swapaxes(Fi, 1, 2), Fk, key_pos[None, :] <= q_pos[:, None])

    out = lax.map(block, (qb, Fb, jnp.arange(nb)))
    return jnp.moveaxis(out, 0, 1).reshape(B, T, H, d)


def fox_decode(q, k_all, v_all, logf_all, past_len):
    Td, L = q.shape[1], k_all.shape[1]
    F = jnp.swapaxes(jnp.cumsum(logf_all, axis=1), 1, 2)
    mask = jnp.arange(L)[None, :] <= (past_len + jnp.arange(Td))[:, None]
    return fox_combine(q, k_all, v_all, F[:, :, past_len:], F, mask)


def gla_chunked(q, k, v, log_a, s0):
    B, T, H, DK = q.shape
    DV = v.shape[-1]
    C = GLA_CHUNK if T % GLA_CHUNK == 0 else T
    n = T // C
    f32 = jnp.float32
    qf = q.astype(f32).reshape(B, n, C, H, DK) * (DK ** -0.5)
    kf = k.astype(f32).reshape(B, n, C, H, DK)
    vf = v.astype(f32).reshape(B, n, C, H, DV)
    b = jnp.cumsum(log_a.reshape(B, n, C, H, DK), axis=2)
    b_last = b[:, :, -1]
    q_dec = qf * jnp.exp(b)
    att = jnp.einsum('bnchd,bnshd->bnhcs', q_dec, kf * jnp.exp(-b))
    att = jnp.where(jnp.tril(jnp.ones((C, C), dtype=bool)), att, 0.0)
    o_intra = jnp.einsum('bnhcs,bnshv->bnchv', att, vf)
    upd = jnp.einsum('bnchd,bnchv->bnhdv', kf * jnp.exp(b_last[:, :, None] - b), vf)

    def step(s, inp):
        dec, u = inp
        return jnp.exp(dec)[..., None] * s + u, s

    s_fin, s_prev = lax.scan(step, s0.astype(f32), (jnp.moveaxis(b_last, 1, 0), jnp.moveaxis(upd, 1, 0)))
    o_inter = jnp.einsum('bnchd,bnhdv->bnchv', q_dec, jnp.moveaxis(s_prev, 0, 1))
    return (o_intra + o_inter).reshape(B, T, H, DV).astype(v.dtype), s_fin


def merge(x, oa, ga, ob, gb, oc, gc, gla_norm_g, w_out):
    B, T, _ = x.shape
    ob = rms_norm(ob, gla_norm_g)
    y = jnp.concatenate([oa.reshape(B, T, W_SB) * jax.nn.silu(ga),
                         ob.reshape(B, T, W_GLA) * jax.nn.silu(gb),
                         oc.reshape(B, T, W_FOX) * jax.nn.silu(gc)], axis=-1)
    return x + y @ w_out


def gather_pages(pool, page_table):
    g = pool[page_table]
    return g.reshape((g.shape[0], g.shape[1] * g.shape[2]) + g.shape[3:])


def setup_inputs(seed: int = 0) -> dict:
    key = jax.random.key(seed)
    ks = jax.random.split(key, 24)
    f32 = jnp.float32
    n_pages = PAST_LEN // PAGE_SIZE
    used = DEC_BATCH * n_pages
    n_pool = used + max(1, used // 4)
    nrm = lambda k, s, sc: jax.random.normal(k, s, f32) * sc
    page_table = jax.random.permutation(ks[0], n_pool)[:used].reshape(DEC_BATCH, n_pages).astype(jnp.int32)
    return {
        "x_prompt": nrm(ks[1], (BATCH, SEQ, D_MODEL), 1.0),
        "x_sample": nrm(ks[2], (DEC_BATCH, DEC_SEQ, D_MODEL), 1.0),
        "cache_sb_k": nrm(ks[3], (DEPTH, n_pool, PAGE_SIZE, H_SB, HEAD_DIM), 1.0),
        "cache_sb_v": nrm(ks[4], (DEPTH, n_pool, PAGE_SIZE, H_SB, HEAD_DIM), 1.0),
        "cache_fox_k": nrm(ks[5], (DEPTH, n_pool, PAGE_SIZE, H_FOX, HEAD_DIM), 1.0),
        "cache_fox_v": nrm(ks[6], (DEPTH, n_pool, PAGE_SIZE, H_FOX, HEAD_DIM), 1.0),
        "cache_fox_logf": jax.nn.log_sigmoid(2.0 + nrm(ks[7], (DEPTH, n_pool, PAGE_SIZE, H_FOX), 0.5)),
        "state_gla": nrm(ks[8], (DEPTH, DEC_BATCH, H_GLA, DK_GLA, DV_GLA), 0.5),
        "page_table": page_table,
        "norm_g": 1.0 + nrm(ks[9], (DEPTH, D_MODEL), 0.02),
        "w_in": nrm(ks[10], (DEPTH, D_MODEL, N_IN), D_MODEL ** -0.5),
        "w_gate_b": nrm(ks[11], (DEPTH, GATE_RANK, H_GLA * DK_GLA), GATE_RANK ** -0.5),
        "b_gate": nrm(ks[12], (DEPTH, H_GLA * DK_GLA), 0.1),
        "b_forget": 2.0 + nrm(ks[13], (DEPTH, H_FOX), 0.1),
        "q_norm_g": 1.0 + nrm(ks[14], (DEPTH, HEAD_DIM), 0.02),
        "k_norm_g": 1.0 + nrm(ks[15], (DEPTH, HEAD_DIM), 0.02),
        "gla_norm_g": 1.0 + nrm(ks[16], (DEPTH, DV_GLA), 0.02),
        "w_out": nrm(ks[17], (DEPTH, MIX_WIDTH, D_MODEL), MIX_WIDTH ** -0.5),
        "norm_f": 1.0 + nrm(ks[18], (D_MODEL,), 0.02),
    }


def reference(x_prompt, x_sample, cache_sb_k, cache_sb_v, cache_fox_k, cache_fox_v, cache_fox_logf,
              state_gla, page_table, norm_g, w_in, w_gate_b, b_gate, b_forget, q_norm_g, k_norm_g,
              gla_norm_g, w_out, norm_f):
    past_len = page_table.shape[1] * cache_sb_k.shape[2]
    hp, hs = x_prompt, x_sample
    B = x_prompt.shape[0]
    sbk_p, sbv_p, fk_p, fv_p, fl_p, gs_p = [], [], [], [], [], []
    sbk_s, sbv_s, fk_s, fv_s, fl_s, gs_s = [], [], [], [], [], []
    for l in range(DEPTH):
        lw = (norm_g[l], w_in[l], w_gate_b[l], b_gate[l], b_forget[l], q_norm_g[l], k_norm_g[l])
        qa, ka, va, ga, qb, kb, vb, gb, lab, qc, kc, vc, gc, lfc = project(hp, *lw)
        oa = sb_prompt(qa, ka, va)
        ob, s_fin = gla_chunked(qb, kb, vb, lab, jnp.zeros((B, H_GLA, DK_GLA, DV_GLA), jnp.float32))
        oc = fox_prompt(qc, kc, vc, lfc)
        hp = merge(hp, oa, ga, ob, gb, oc, gc, gla_norm_g[l], w_out[l])
        sbk_p.append(ka); sbv_p.append(va); fk_p.append(kc); fv_p.append(vc); fl_p.append(lfc); gs_p.append(s_fin)
        qa, ka, va, ga, qb, kb, vb, gb, lab, qc, kc, vc, gc, lfc = project(hs, *lw)
        ka_all = jnp.concatenate([gather_pages(cache_sb_k[l], page_table).astype(ka.dtype), ka], axis=1)
        va_all = jnp.concatenate([gather_pages(cache_sb_v[l], page_table).astype(va.dtype), va], axis=1)
        oa = sb_decode(qa, ka_all, va_all, past_len)
        ob, s_new = gla_chunked(qb, kb, vb, lab, state_gla[l])
        kc_all = jnp.concatenate([gather_pages(cache_fox_k[l], page_table).astype(kc.dtype), kc], axis=1)
        vc_all = jnp.concatenate([gather_pages(cache_fox_v[l], page_table).astype(vc.dtype), vc], axis=1)
        lf_all = jnp.concatenate([gather_pages(cache_fox_logf[l], page_table).astype(jnp.float32), lfc], axis=1)
        oc = fox_decode(qc, kc_all, vc_all, lf_all, past_len)
        hs = merge(hs, oa, ga, ob, gb, oc, gc, gla_norm_g[l], w_out[l])
        sbk_s.append(ka); sbv_s.append(va); fk_s.append(kc); fv_s.append(vc); fl_s.append(lfc); gs_s.append(s_new)
    y_prompt = rms_norm(hp, norm_f)
    y_sample = rms_norm(hs, norm_f)
    return (y_prompt, y_sample,
            jnp.stack(sbk_p), jnp.stack(sbv_p), jnp.stack(fk_p), jnp.stack(fv_p), jnp.stack(fl_p), jnp.stack(gs_p),
            jnp.stack(sbk_s), jnp.stack(sbv_s), jnp.stack(fk_s), jnp.stack(fv_s), jnp.stack(fl_s), jnp.stack(gs_s))
```

```python
import functools

import jax
import jax.numpy as jnp
from jax import lax
from jax.experimental import pallas as pl
from jax.experimental.pallas import tpu as pltpu

F32 = jnp.float32
BF16 = jnp.bfloat16

LANES = 128
SUBLANES = 8
RMS_EPS = 1e-6
GATE_TAU = 16.0
GLA_CHUNK = 64
NEG_BIG = -1e30

SIDE_FC = 0
SIDE_AB = 8


def _dot(a, b):
    return jnp.dot(a, b, preferred_element_type=F32)


def _dot_nt(a, b):
    return lax.dot_general(a, b, (((1,), (1,)), ((), ())), preferred_element_type=F32)


def _dot_tn(a, b):
    return lax.dot_general(a, b, (((0,), (0,)), ((), ())), preferred_element_type=F32)


def _softplus(z):
    return jnp.maximum(z, 0.0) + jnp.log(1.0 + jnp.exp(-jnp.abs(z)))


def _log_sigmoid(z):
    return -_softplus(-z)


def _silu(g):
    return g / (1.0 + jnp.exp(-g))


def _rms(x, g):
    ms = jnp.mean(x * x, axis=-1, keepdims=True)
    return x * lax.rsqrt(ms + RMS_EPS) * g


def _split_bf16(x, parts):
    out = []
    r = x
    for _ in range(parts):
        p = r.astype(BF16)
        out.append(p)
        r = r - p.astype(F32)
    return out


def _dot_split(x, m, parts, left=False):
    acc = None
    for p in _split_bf16(x, parts):
        t = _dot(m, p) if left else _dot(p, m)
        acc = t if acc is None else acc + t
    return acc


def _iota(shape, dim):
    return lax.broadcasted_iota(jnp.int32, shape, dim)


def _proj_kernel(l_ref, x_ref, g_ref, w_ref, qg_ref, kg_ref, o_ref, h_ref, *, jq, jk):
    j = pl.program_id(1)

    @pl.when(j == 0)
    def _():
        h_ref[...] = _rms(x_ref[...], g_ref[...]).astype(BF16)

    acc = _dot(h_ref[...], w_ref[...])

    def headnorm(a, g):
        outs = [_rms(a[:, h * LANES:(h + 1) * LANES], g) for h in range(a.shape[1] // LANES)]
        return jnp.concatenate(outs, axis=1)

    is_q = j == jq
    is_k = j == jk

    @pl.when(is_q)
    def _():
        o_ref[...] = headnorm(acc, qg_ref[...])

    @pl.when(is_k)
    def _():
        o_ref[...] = headnorm(acc, kg_ref[...])

    @pl.when(jnp.logical_not(jnp.logical_or(is_q, is_k)))
    def _():
        o_ref[...] = acc


def _proj(l_arr, x, norm_g, w_main, qg, kg, *, tm, tn, jq, jk):
    m, d = x.shape
    n = w_main.shape[2]
    grid = (m // tm, n // tn)
    return pl.pallas_call(
        functools.partial(_proj_kernel, jq=jq, jk=jk),
        grid_spec=pltpu.PrefetchScalarGridSpec(
            num_scalar_prefetch=1,
            grid=grid,
            in_specs=[
                pl.BlockSpec((tm, d), lambda i, j, l: (i, 0)),
                pl.BlockSpec((None, 1, d), lambda i, j, l: (l[0], 0, 0)),
                pl.BlockSpec((None, d, tn), lambda i, j, l: (l[0], 0, j)),
                pl.BlockSpec((None, 1, LANES), lambda i, j, l: (l[0], 0, 0)),
                pl.BlockSpec((None, 1, LANES), lambda i, j, l: (l[0], 0, 0)),
            ],
            out_specs=pl.BlockSpec((tm, tn), lambda i, j, l: (i, j)),
            scratch_shapes=[pltpu.VMEM((tm, d), BF16)],
        ),
        out_shape=jax.ShapeDtypeStruct((m, n), F32),
        compiler_params=pltpu.CompilerParams(dimension_semantics=("arbitrary", "arbitrary")),
        name="proj",
    )(l_arr, x, norm_g, w_main, qg, kg)


def _side_kernel(l_ref, x_ref, g_ref, ws_ref, wgb_ref, bg_ref, bf_ref, la_ref, lf_ref, *rest,
                 n_heads, with_cumsum):
    x = x_ref[...]
    h = _rms(x, g_ref[...]).astype(BF16)
    s = _dot(h, ws_ref[...])
    la_ref[...] = _log_sigmoid(_dot(s.astype(BF16), wgb_ref[...]) + bg_ref[...]) / GATE_TAU
    lf = _log_sigmoid(s + bf_ref[...])
    lf_ref[...] = lf
    if with_cumsum:
        fq_ref, ft_ref, carry_ref = rest
        ta = x.shape[0]

        @pl.when(pl.program_id(1) == 0)
        def _():
            carry_ref[...] = jnp.zeros_like(carry_ref)

        lower = (_iota((ta, ta), 0) >= _iota((ta, ta), 1)).astype(BF16)
        f = _dot_split(lf, lower, 3, left=True) + carry_ref[...]
        carry_ref[...] = f[ta - 1:ta, :]
        ft = f.T
        for hh in range(n_heads):
            c = SIDE_FC + hh
            fq_ref[:, hh * LANES:(hh + 1) * LANES] = jnp.broadcast_to(f[:, c:c + 1], (ta, LANES))
            ft_ref[hh] = ft[c:c + 1, :]


def _side(l_arr, x, norm_g, w_side, wgb, bg, bfg, *, batch, ta, n_heads, with_cumsum):
    m, d = x.shape
    t = m // batch
    nt = t // ta
    n_la = wgb.shape[2]
    in_specs = [
        pl.BlockSpec((ta, d), lambda b, i, l: (b * nt + i, 0)),
        pl.BlockSpec((None, 1, d), lambda b, i, l: (l[0], 0, 0)),
        pl.BlockSpec((None, d, LANES), lambda b, i, l: (l[0], 0, 0)),
        pl.BlockSpec((None, LANES, n_la), lambda b, i, l: (l[0], 0, 0)),
        pl.BlockSpec((None, 1, n_la), lambda b, i, l: (l[0], 0, 0)),
        pl.BlockSpec((None, 1, LANES), lambda b, i, l: (l[0], 0, 0)),
    ]
    out_specs = [
        pl.BlockSpec((ta, n_la), lambda b, i, l: (b * nt + i, 0)),
        pl.BlockSpec((ta, LANES), lambda b, i, l: (b * nt + i, 0)),
    ]
    out_shape = [jax.ShapeDtypeStruct((m, n_la), F32), jax.ShapeDtypeStruct((m, LANES), F32)]
    scratch = []
    if with_cumsum:
        out_specs += [
            pl.BlockSpec((ta, n_heads * LANES), lambda b, i, l: (b * nt + i, 0)),
            pl.BlockSpec((None, n_heads, 1, ta), lambda b, i, l: (b, 0, 0, i)),
        ]
        out_shape += [jax.ShapeDtypeStruct((m, n_heads * LANES), F32),
                      jax.ShapeDtypeStruct((batch, n_heads, 1, t), F32)]
        scratch = [pltpu.VMEM((1, LANES), F32)]
    return pl.pallas_call(
        functools.partial(_side_kernel, n_heads=n_heads, with_cumsum=with_cumsum),
        grid_spec=pltpu.PrefetchScalarGridSpec(
            num_scalar_prefetch=1, grid=(batch, nt), in_specs=in_specs, out_specs=out_specs,
            scratch_shapes=scratch),
        out_shape=out_shape,
        compiler_params=pltpu.CompilerParams(dimension_semantics=("arbitrary", "arbitrary")),
        name="side",
    )(l_arr, x, norm_g, w_side, wgb, bg, bfg)


def _sb_prompt_kernel(q_ref, k_ref, v_ref, g_ref, o_ref, *, tb, scale):
    i = pl.program_id(2)
    q = q_ref[...].astype(BF16)
    row = _iota((tb, tb), 0)
    col = _iota((tb, tb), 1)
    later = (row > col).astype(BF16)
    valid = col < row

    def scores(j):
        off = pl.multiple_of(j * tb, tb)
        k = k_ref[pl.ds(off, tb), :].astype(BF16)
        v = v_ref[pl.ds(off, tb), :].astype(BF16)
        return _dot_nt(q, k) * scale, v

    z, v = scores(i)
    l1m = jnp.where(valid, -_softplus(z), 0.0)
    cum = _dot_split(l1m, later, 2)
    w = jnp.where(valid, jnp.exp(z + l1m + cum), 0.0)
    acc = _dot(w.astype(BF16), v)
    carry = jnp.sum(l1m, axis=1, keepdims=True)

    def body(jj, c):
        carry, acc = c
        z, v = scores(i - 1 - jj)
        l1m = -_softplus(z)
        cum = _dot_split(l1m, later, 2) + carry
        w = jnp.exp(z + l1m + cum)
        return carry + jnp.sum(l1m, axis=1, keepdims=True), acc + _dot(w.astype(BF16), v)

    carry, acc = lax.fori_loop(0, i, body, (carry, acc))
    o_ref[...] = (acc * _silu(g_ref[...])).astype(o_ref.dtype)


def _sb_prompt(p, *, batch, n_heads, tb, cq, ck, cv, cg):
    m = p.shape[0]
    t = m // batch
    nq = t // tb
    return pl.pallas_call(
        functools.partial(_sb_prompt_kernel, tb=tb, scale=LANES ** -0.5),
        grid=(batch, n_heads, nq),
        in_specs=[
            pl.BlockSpec((tb, LANES), lambda b, h, i: (b * nq + i, cq + h)),
            pl.BlockSpec((t, LANES), lambda b, h, i: (b, ck + h)),
            pl.BlockSpec((t, LANES), lambda b, h, i: (b, cv + h)),
            pl.BlockSpec((tb, LANES), lambda b, h, i: (b * nq + i, cg + h)),
        ],
        out_specs=pl.BlockSpec((tb, LANES), lambda b, h, i: (b * nq + i, h)),
        out_shape=jax.ShapeDtypeStruct((m, n_heads * LANES), BF16),
        compiler_params=pltpu.CompilerParams(
            dimension_semantics=("arbitrary", "arbitrary", "arbitrary")),
        name="sb_prompt",
    )(p, p, p, p)


def _fox_prompt_kernel(q_ref, k_ref, v_ref, g_ref, fq_ref, ft_ref, o_ref, *, tb, scale):
    i = pl.program_id(2)
    q = q_ref[...].astype(BF16)
    fq = fq_ref[...]
    if tb != LANES:
        fq = jnp.broadcast_to(fq[:, :1], (tb, tb))
    valid = _iota((tb, tb), 1) <= _iota((tb, tb), 0)

    def scores(j):
        off = pl.multiple_of(j * tb, tb)
        k = k_ref[pl.ds(off, tb), :].astype(BF16)
        v = v_ref[pl.ds(off, tb), :].astype(BF16)
        fk = ft_ref[:, pl.ds(off, tb)]
        return _dot_nt(q, k) * scale + (fq - fk), v

    z, v = scores(i)
    z = jnp.where(valid, z, NEG_BIG)
    m = jnp.max(z, axis=1, keepdims=True)
    pr = jnp.exp(z - m)
    l = jnp.sum(pr, axis=1, keepdims=True)
    acc = _dot(pr.astype(BF16), v)

    def body(jj, c):
        m, l, acc = c
        z, v = scores(jj)
        m_new = jnp.maximum(m, jnp.max(z, axis=1, keepdims=True))
        alpha = jnp.exp(m - m_new)
        pr = jnp.exp(z - m_new)
        l = alpha * l + jnp.sum(pr, axis=1, keepdims=True)
        acc = alpha * acc + _dot(pr.astype(BF16), v)
        return m_new, l, acc

    m, l, acc = lax.fori_loop(0, i, body, (m, l, acc))
    o_ref[...] = ((acc / l) * _silu(g_ref[...])).astype(o_ref.dtype)


def _fox_prompt(p, fq, ft, *, batch, n_heads, tb, cq, ck, cv, cg):
    m = p.shape[0]
    t = m // batch
    nq = t // tb
    return pl.pallas_call(
        functools.partial(_fox_prompt_kernel, tb=tb, scale=LANES ** -0.5),
        grid=(batch, n_heads, nq),
        in_specs=[
            pl.BlockSpec((tb, LANES), lambda b, h, i: (b * nq + i, cq + h)),
            pl.BlockSpec((t, LANES), lambda b, h, i: (b, ck + h)),
            pl.BlockSpec((t, LANES), lambda b, h, i: (b, cv + h)),
            pl.BlockSpec((tb, LANES), lambda b, h, i: (b * nq + i, cg + h)),
            pl.BlockSpec((tb, LANES), lambda b, h, i: (b * nq + i, h)),
            pl.BlockSpec((None, None, 1, t), lambda b, h, i: (b, h, 0, 0)),
        ],
        out_specs=pl.BlockSpec((tb, LANES), lambda b, h, i: (b * nq + i, h)),
        out_shape=jax.ShapeDtypeStruct((m, n_heads * LANES), BF16),
        compiler_params=pltpu.CompilerParams(
            dimension_semantics=("arbitrary", "arbitrary", "arbitrary")),
        name="fox_prompt",
    )(p, p, p, p, fq, ft)


def _gla_prompt_kernel(l_ref, q_ref, k_ref, v_ref, g_ref, la_ref, ng_ref, y_ref, s_ref, st_ref, *,
                       tg, n_pairs, scale):
    c_len = GLA_CHUNK
    half = LANES // 2
    ti = pl.program_id(1)

    @pl.when(ti == 0)
    def _():
        st_ref[...] = jnp.zeros_like(st_ref)

    lower = (_iota((c_len, c_len), 0) >= _iota((c_len, c_len), 1)).astype(BF16)
    lane = _iota((c_len, LANES), 1)
    first = lane < half
    r2 = _iota((2 * c_len, 2 * c_len), 0)
    c2 = _iota((2 * c_len, 2 * c_len), 1)
    sh = c_len.bit_length() - 1
    att_mask = jnp.logical_and((r2 >> sh) == (c2 >> sh), (c2 & (c_len - 1)) <= (r2 & (c_len - 1)))
    lane_sq = _iota((LANES, LANES), 1) < half
    ng = ng_ref[...]

    for c in range(tg // c_len):
        rows = slice(c * c_len, (c + 1) * c_len)
        la = la_ref[rows, :]
        b = _dot_split(la, lower, 3, left=True)
        b_last = b[c_len - 1:c_len, :]
        qd = q_ref[rows, :] * scale * jnp.exp(b)
        kk = k_ref[rows, :]
        kd = kk * jnp.exp(-b)
        ku = kk * jnp.exp(b_last - b)
        e_last = jnp.exp(b_last)
        for pr in range(n_pairs):
            ls = slice(pr * LANES, (pr + 1) * LANES)
            qd_p = qd[:, ls]
            qs = jnp.concatenate([jnp.where(first, qd_p, 0.0), jnp.where(first, 0.0, qd_p)],
                                 axis=0).astype(BF16)
            kd_p = kd[:, ls].astype(BF16)
            kd2 = jnp.concatenate([kd_p, kd_p], axis=0)
            att = jnp.where(att_mask, _dot_nt(qs, kd2), 0.0).astype(BF16)
            v_pair = v_ref[rows, 2 * pr * LANES:(2 * pr + 2) * LANES].astype(BF16)
            v_stack = jnp.concatenate([v_pair[:, :LANES], v_pair[:, LANES:]], axis=0)
            st = st_ref[pr]
            o = _dot(att, v_stack) + _dot_nt(qs, st.astype(BF16))
            upd = _dot_tn(v_pair, ku[:, ls].astype(BF16))
            st_ref[pr] = st * e_last[:, ls] + jnp.where(lane_sq, upd[:LANES], upd[LANES:])
            for s in range(2):
                hh = 2 * pr + s
                cs = slice(hh * LANES, (hh + 1) * LANES)
                oh = _rms(o[s * c_len:(s + 1) * c_len], ng)
                y_ref[rows, cs] = (oh * _silu(g_ref[rows, cs])).astype(y_ref.dtype)

    @pl.when(ti == pl.num_programs(1) - 1)
    def _():
        for pr in range(n_pairs):
            s_t = st_ref[pr].T
            s_ref[2 * pr] = s_t[:half]
            s_ref[2 * pr + 1] = s_t[half:]


def _gla_prompt(l_arr, p, la, ng, *, batch, n_heads, dk, dv, tg, cq, ck, cv, cg):
    m = p.shape[0]
    t = m // batch
    nt = t // tg
    wk = n_heads * dk
    wv = n_heads * dv
    n_pairs = n_heads // 2
    return pl.pallas_call(
        functools.partial(_gla_prompt_kernel, tg=tg, n_pairs=n_pairs, scale=dk ** -0.5),
        grid_spec=pltpu.PrefetchScalarGridSpec(
            num_scalar_prefetch=1,
            grid=(batch, nt),
            in_specs=[
                pl.BlockSpec((tg, wk), lambda b, i, l: (b * nt + i, cq)),
                pl.BlockSpec((tg, wk), lambda b, i, l: (b * nt + i, ck)),
                pl.BlockSpec((tg, wv), lambda b, i, l: (b * nt + i, cv)),
                pl.BlockSpec((tg, wv), lambda b, i, l: (b * nt + i, cg)),
                pl.BlockSpec((tg, wk), lambda b, i, l: (b * nt + i, 0)),
                pl.BlockSpec((None, 1, dv), lambda b, i, l: (l[0], 0, 0)),
            ],
            out_specs=[
                pl.BlockSpec((tg, wv), lambda b, i, l: (b * nt + i, 0)),
                pl.BlockSpec((None, n_heads, dk, dv), lambda b, i, l: (b, 0, 0, 0)),
            ],
            scratch_shapes=[pltpu.VMEM((n_pairs, LANES, LANES), F32)],
        ),
        out_shape=[jax.ShapeDtypeStruct((m, wv), BF16),
                   jax.ShapeDtypeStruct((batch, n_heads, dk, dv), F32)],
        compiler_params=pltpu.CompilerParams(dimension_semantics=("arbitrary", "arbitrary")),
        name="gla_prompt",
    )(l_arr, p, p, p, p, la, ng)


def _out_kernel(l_ref, x_ref, ya_ref, yb_ref, yc_ref, w0, w1, w2, w3, nf_ref, o_ref, *, final):
    wq = w0.shape[0]
    yb = yb_ref[...].astype(BF16)
    acc = x_ref[...] + _dot(ya_ref[...].astype(BF16), w0[...])
    acc = acc + _dot(yb[:, :wq], w1[...]) + _dot(yb[:, wq:], w2[...])
    acc = acc + _dot(yc_ref[...].astype(BF16), w3[...])
    o_ref[...] = _rms(acc, nf_ref[...]) if final else acc


def _out_proj(l_arr, x, ya, yb, yc, w_out, nf, *, tm, final):
    m, d = x.shape
    wq = ya.shape[1]
    assert yb.shape[1] == 2 * wq and yc.shape[1] == wq and w_out.shape[1] == 4 * wq
    wspec = lambda r: pl.BlockSpec((None, wq, d), lambda i, l, r=r: (l[0], r, 0))
    return pl.pallas_call(
        functools.partial(_out_kernel, final=final),
        grid_spec=pltpu.PrefetchScalarGridSpec(
            num_scalar_prefetch=1,
            grid=(m // tm,),
            in_specs=[
                pl.BlockSpec((tm, d), lambda i, l: (i, 0)),
                pl.BlockSpec((tm, wq), lambda i, l: (i, 0)),
                pl.BlockSpec((tm, 2 * wq), lambda i, l: (i, 0)),
                pl.BlockSpec((tm, wq), lambda i, l: (i, 0)),
                wspec(0), wspec(1), wspec(2), wspec(3),
                pl.BlockSpec((1, d), lambda i, l: (0, 0)),
            ],
            out_specs=pl.BlockSpec((tm, d), lambda i, l: (i, 0)),
        ),
        out_shape=jax.ShapeDtypeStruct((m, d), F32),
        compiler_params=pltpu.CompilerParams(dimension_semantics=("arbitrary",)),
        name="out_proj",
    )(l_arr, x, ya, yb, yc, w_out, w_out, w_out, w_out, nf)


def _rev_excl_cumsum(x, later, parts):
    r, n = x.shape
    nb = n // LANES
    xs = jnp.concatenate([x[:, i * LANES:(i + 1) * LANES] for i in range(nb)], axis=0)
    cs = _dot_split(xs, later, parts)
    tot = jnp.sum(xs, axis=1, keepdims=True)
    run = jnp.zeros((r, 1), F32)
    blocks = [None] * nb
    for i in range(nb - 1, -1, -1):
        blocks[i] = cs[i * r:(i + 1) * r] + run
        run = run + tot[i * r:(i + 1) * r]
    return jnp.concatenate(blocks, axis=1), run


def _block_diag_rows(row_vec, n_rows):
    w = row_vec.shape[1]
    keep = (_iota((n_rows, w), 1) >> (LANES.bit_length() - 1)) == _iota((n_rows, w), 0)
    return jnp.where(keep, jnp.broadcast_to(row_vec, (n_rows, w)), 0.0), keep


def _page_copies(pt_ref, l, bb, gg, slot, srcs, bufs, sem, *, n_groups, g_pages, page):
    base = (n_groups - 1 - gg) * g_pages
    out = []
    for r in range(g_pages):
        pg = pt_ref[bb, base + r]
        for si, (src, buf) in enumerate(zip(srcs, bufs)):
            rows = src.shape[2]
            out.append(pltpu.make_async_copy(
                src.at[l, pg], buf.at[slot, pl.ds(r * rows, rows)], sem.at[si, slot]))
    return out


def _paged_pipeline(pt_ref, l, srcs, bufs, sem, *, n_groups, g_pages, page):
    bb = pl.program_id(0)
    gg = pl.program_id(1)
    step = bb * n_groups + gg
    total = pl.num_programs(0) * n_groups
    slot = lax.rem(step, 2)
    mk = functools.partial(_page_copies, pt_ref, l, srcs=srcs, bufs=bufs, sem=sem,
                           n_groups=n_groups, g_pages=g_pages, page=page)

    @pl.when(step == 0)
    def _():
        for cp in mk(bb, gg, slot):
            cp.start()

    @pl.when(step + 1 < total)
    def _():
        wrap = gg + 1 == n_groups
        nb = jnp.where(wrap, bb + 1, bb)
        ng = jnp.where(wrap, 0, gg + 1)
        for cp in mk(nb, ng, 1 - slot):
            cp.start()

    for cp in mk(bb, gg, slot):
        cp.wait()
    return slot


def _sb_decode_kernel(pt_ref, l_ref, p_ref, kc_ref, vc_ref, o_ref, kbuf, vbuf, sem, acc_ref,
                      carry_ref, *, n_groups, g_pages, page, n_heads, scale, cq, cg):
    bb = pl.program_id(0)
    gg = pl.program_id(1)
    w = n_heads * LANES
    slot = _paged_pipeline(pt_ref, l_ref[0], (kc_ref, vc_ref), (kbuf, vbuf), sem,
                           n_groups=n_groups, g_pages=g_pages, page=page)

    @pl.when(gg == 0)
    def _():
        acc_ref[...] = jnp.zeros_like(acc_ref)
        carry_ref[...] = jnp.zeros_like(carry_ref)

    q_row = p_ref[:,cq * LANES:cq * LANES + w]
    qbd, keep = _block_diag_rows(q_row, SUBLANES)
    later = (_iota((LANES, LANES), 0) > _iota((LANES, LANES), 1)).astype(BF16)
    z = _dot_nt(qbd.astype(BF16), kbuf[slot].astype(BF16)) * scale
    l1m = -_softplus(z)
    cum, tot = _rev_excl_cumsum(l1m, later, 2)
    wgt = jnp.exp(z + l1m + cum + carry_ref[...])
    acc_ref[...] += _dot(wgt.astype(BF16), vbuf[slot].astype(BF16))
    carry_ref[...] += tot

    @pl.when(gg == n_groups - 1)
    def _():
        o = jnp.sum(jnp.where(keep, acc_ref[...], 0.0), axis=0, keepdims=True)
        gate = p_ref[:,cg * LANES:cg * LANES + w]
        o_ref[...] =o * _silu(gate)


def _sb_decode(page_table, l_arr, p, kc, vc, *, g_pages, cq, cg):
    n_seq, n_pages = page_table.shape
    page, w = kc.shape[2], kc.shape[3]
    n_heads = w // LANES
    n_groups = n_pages // g_pages
    tok = g_pages * page
    return pl.pallas_call(
        functools.partial(_sb_decode_kernel, n_groups=n_groups, g_pages=g_pages, page=page,
                          n_heads=n_heads, scale=LANES ** -0.5, cq=cq, cg=cg),
        grid_spec=pltpu.PrefetchScalarGridSpec(
            num_scalar_prefetch=2,
            grid=(n_seq, n_groups),
            in_specs=[
                pl.BlockSpec((None, 1, p.shape[2]), lambda b, g, pt, l: (b, 0, 0)),
                pl.BlockSpec(memory_space=pl.ANY),
                pl.BlockSpec(memory_space=pl.ANY),
            ],
            out_specs=pl.BlockSpec((None, 1, w), lambda b, g, pt, l: (b, 0, 0)),
            scratch_shapes=[
                pltpu.VMEM((2, tok, w), F32),
                pltpu.VMEM((2, tok, w), F32),
                pltpu.SemaphoreType.DMA((2, 2)),
                pltpu.VMEM((SUBLANES, w), F32),
                pltpu.VMEM((SUBLANES, 1), F32),
            ],
        ),
        out_shape=jax.ShapeDtypeStruct((n_seq, 1, w), F32),
        compiler_params=pltpu.CompilerParams(dimension_semantics=("arbitrary", "arbitrary")),
        name="sb_decode",
    )(page_table, l_arr, p, kc, vc)


def _fox_decode_kernel(pt_ref, l_ref, p_ref, lfn_ref, kc_ref, vc_ref, lc_ref, o_ref, kbuf, vbuf,
                       lbuf, sem, acc_ref, m_ref, l_sum_ref, carry_ref, *, n_groups, g_pages, page,
                       n_heads, scale, cq, ck, cv, cg):
    bb = pl.program_id(0)
    gg = pl.program_id(1)
    w = n_heads * LANES
    slot = _paged_pipeline(pt_ref, l_ref[0], (kc_ref, vc_ref, lc_ref), (kbuf, vbuf, lbuf), sem,
                           n_groups=n_groups, g_pages=g_pages, page=page)

    q_row = p_ref[:,cq * LANES:cq * LANES + w]
    qbd, keep = _block_diag_rows(q_row, SUBLANES)

    @pl.when(gg == 0)
    def _():
        k_new = p_ref[:,ck * LANES:ck * LANES + w]
        v_new = p_ref[:,cv * LANES:cv * LANES + w]
        m_ref[...] = jnp.sum(qbd * k_new, axis=1, keepdims=True) * scale
        l_sum_ref[...] = jnp.ones_like(l_sum_ref)
        acc_ref[...] = jnp.broadcast_to(v_new, acc_ref.shape)
        lf_row = lfn_ref[...]
        pick = _iota((SUBLANES, LANES), 1) == _iota((SUBLANES, LANES), 0) + SIDE_FC
        carry_ref[...] = jnp.sum(jnp.where(pick, jnp.broadcast_to(lf_row, (SUBLANES, LANES)), 0.0),
                                 axis=1, keepdims=True)

    later = (_iota((LANES, LANES), 0) > _iota((LANES, LANES), 1)).astype(BF16)
    lf = jnp.concatenate([lbuf[slot, pl.ds(r * SUBLANES, SUBLANES), :] for r in range(g_pages)],
                         axis=1)
    bias, tot = _rev_excl_cumsum(lf, later, 3)
    z = _dot_nt(qbd.astype(BF16), kbuf[slot].astype(BF16)) * scale + (bias + carry_ref[...])
    m_old = m_ref[...]
    m_new = jnp.maximum(m_old, jnp.max(z, axis=1, keepdims=True))
    alpha = jnp.exp(m_old - m_new)
    pr = jnp.exp(z - m_new)
    l_sum_ref[...] = alpha * l_sum_ref[...] + jnp.sum(pr, axis=1, keepdims=True)
    acc_ref[...] = alpha * acc_ref[...] + _dot(pr.astype(BF16), vbuf[slot].astype(BF16))
    m_ref[...] = m_new
    carry_ref[...] += tot

    @pl.when(gg == n_groups - 1)
    def _():
        o = jnp.sum(jnp.where(keep, acc_ref[...] / l_sum_ref[...], 0.0), axis=0, keepdims=True)
        gate = p_ref[:,cg * LANES:cg * LANES + w]
        o_ref[...] =o * _silu(gate)


def _fox_decode(page_table, l_arr, p, lf_new, kc, vc, lc, *, g_pages, cq, ck, cv, cg):
    n_seq, n_pages = page_table.shape
    page, w = kc.shape[2], kc.shape[3]
    n_heads = w // LANES
    n_groups = n_pages // g_pages
    tok = g_pages * page
    return pl.pallas_call(
        functools.partial(_fox_decode_kernel, n_groups=n_groups, g_pages=g_pages, page=page,
                          n_heads=n_heads, scale=LANES ** -0.5, cq=cq, ck=ck, cv=cv, cg=cg),
        grid_spec=pltpu.PrefetchScalarGridSpec(
            num_scalar_prefetch=2,
            grid=(n_seq, n_groups),
            in_specs=[
                pl.BlockSpec((None, 1, p.shape[2]), lambda b, g, pt, l: (b, 0, 0)),
                pl.BlockSpec((None, 1, lf_new.shape[2]), lambda b, g, pt, l: (b, 0, 0)),
                pl.BlockSpec(memory_space=pl.ANY),
                pl.BlockSpec(memory_space=pl.ANY),
                pl.BlockSpec(memory_space=pl.ANY),
            ],
            out_specs=pl.BlockSpec((None, 1, w), lambda b, g, pt, l: (b, 0, 0)),
            scratch_shapes=[
                pltpu.VMEM((2, tok, w), F32),
                pltpu.VMEM((2, tok, w), F32),
                pltpu.VMEM((2, g_pages * SUBLANES, LANES), F32),
                pltpu.SemaphoreType.DMA((3, 2)),
                pltpu.VMEM((SUBLANES, w), F32),
                pltpu.VMEM((SUBLANES, 1), F32),
                pltpu.VMEM((SUBLANES, 1), F32),
                pltpu.VMEM((SUBLANES, 1), F32),
            ],
        ),
        out_shape=jax.ShapeDtypeStruct((n_seq, 1, w), F32),
        compiler_params=pltpu.CompilerParams(dimension_semantics=("arbitrary", "arbitrary")),
        name="fox_decode",
    )(page_table, l_arr, p, lf_new, kc, vc, lc)


def _gla_decode_kernel(l_ref, p_ref, la_ref, s0_ref, ng_ref, y_ref, s_ref, *, n_heads, dk, dv,
                       scale, cq, ck, cv, cg):
    bb = pl.program_id(0)
    wk = n_heads * dk

    q_row = p_ref[:,cq * LANES:cq * LANES + wk]
    k_row = p_ref[:,ck * LANES:ck * LANES + wk]
    e_row = jnp.exp(la_ref[...])
    lane = _iota((dk, wk), 1)
    sub = _iota((dk, wk), 0)
    ng = ng_ref[...]

    def column(row_vec, hh):
        pick = lane == sub + hh * dk
        return jnp.sum(jnp.where(pick, jnp.broadcast_to(row_vec, (dk, wk)), 0.0),
                       axis=1, keepdims=True)

    for hh in range(n_heads):
        cs = slice(hh * dv, (hh + 1) * dv)
        v_row = p_ref[:,cv * LANES + hh * dv:cv * LANES + (hh + 1) * dv]
        s_new = column(e_row, hh) * s0_ref[hh] + column(k_row, hh) * v_row
        s_ref[hh] = s_new
        o = jnp.sum((column(q_row, hh) * scale) * s_new, axis=0, keepdims=True)
        gate = p_ref[:,cg * LANES + hh * dv:cg * LANES + (hh + 1) * dv]
        y_ref[:, cs] = _rms(o, ng) * _silu(gate)


def _gla_decode(l_arr, p, la, state, ng, *, cq, ck, cv, cg):
    _, n_seq, n_heads, dk, dv = state.shape
    return pl.pallas_call(
        functools.partial(_gla_decode_kernel, n_heads=n_heads, dk=dk, dv=dv, scale=dk ** -0.5,
                          cq=cq, ck=ck, cv=cv, cg=cg),
        grid_spec=pltpu.PrefetchScalarGridSpec(
            num_scalar_prefetch=1,
            grid=(n_seq,),
            in_specs=[
                pl.BlockSpec((None, 1, p.shape[2]), lambda b, l: (b, 0, 0)),
                pl.BlockSpec((None, 1, la.shape[2]), lambda b, l: (b, 0, 0)),
                pl.BlockSpec((None, None, n_heads, dk, dv), lambda b, l: (l[0], b, 0, 0, 0)),
                pl.BlockSpec((None, 1, dv), lambda b, l: (l[0], 0, 0)),
            ],
            out_specs=[
                pl.BlockSpec((None, 1, n_heads * dv), lambda b, l: (b, 0, 0)),
                pl.BlockSpec((None, n_heads, dk, dv), lambda b, l: (b, 0, 0, 0)),
            ],
        ),
        out_shape=[jax.ShapeDtypeStruct((n_seq, 1, n_heads * dv), F32),
                   jax.ShapeDtypeStruct((n_seq, n_heads, dk, dv), F32)],
        compiler_params=pltpu.CompilerParams(dimension_semantics=("arbitrary",)),
        name="gla_decode",
    )(l_arr, p, la, state, ng)


DEC_ROWS = 16
PROJ_TM = 1024
PROJ_TN = 512
SIDE_TA = 256
ATT_TB = 128
GLA_TG = 256
OUT_TM = 512
PAGES_PER_STEP = 8


def kernel(x_prompt, x_sample, cache_sb_k, cache_sb_v, cache_fox_k, cache_fox_v, cache_fox_logf,
           state_gla, page_table, norm_g, w_in, w_gate_b, b_gate, b_forget, q_norm_g, k_norm_g,
           gla_norm_g, w_out, norm_f):
    batch, seq, d = x_prompt.shape
    n_seq = x_sample.shape[0]
    depth, n_pool, page, h_sb, hd = cache_sb_k.shape
    h_fox = cache_fox_k.shape[3]
    _, _, h_gla, dk, dv = state_gla.shape
    rank = w_gate_b.shape[1]
    assert hd == LANES and dv == LANES and 2 * dk == LANES and x_sample.shape[1] == 1
    w_sb, w_fox, w_gk, w_gv = h_sb * hd, h_fox * hd, h_gla * dk, h_gla * dv
    assert w_sb == w_fox == w_gk and w_gv == 2 * w_sb

    sizes = (w_sb,) * 4 + (w_gk, w_gk, w_gv, w_gv, rank) + (w_fox,) * 4 + (h_fox,)
    offs = [0]
    for s in sizes:
        offs.append(offs[-1] + s)
    assert offs[-1] == w_in.shape[2]
    o_ab, o_qc, o_fc = offs[8], offs[9], offs[13]
    n_main = o_ab + (o_fc - o_qc)
    c_qa, c_ka, c_va, c_ga = (offs[i] // LANES for i in range(4))
    c_qb, c_kb, c_vb, c_gb = (offs[i] // LANES for i in range(4, 8))
    c_qc, c_kc, c_vc, c_gc = ((offs[i] - rank) // LANES for i in range(9, 13))

    w_main = jnp.concatenate([w_in[:, :, :o_ab], w_in[:, :, o_qc:o_fc]], axis=2).astype(BF16)
    w_side = jnp.zeros((depth, d, LANES), F32)
    w_side = w_side.at[:, :, SIDE_FC:SIDE_FC + h_fox].set(w_in[:, :, o_fc:])
    w_side = w_side.at[:, :, SIDE_AB:SIDE_AB + rank].set(w_in[:, :, o_ab:o_qc]).astype(BF16)
    wgb = jnp.zeros((depth, LANES, w_gk), F32).at[:, SIDE_AB:SIDE_AB + rank].set(w_gate_b)
    wgb = wgb.astype(BF16)
    bg = b_gate.reshape(depth, 1, w_gk)
    bfg = jnp.zeros((depth, 1, LANES), F32).at[:, 0, SIDE_FC:SIDE_FC + h_fox].set(b_forget)
    w_o = w_out.astype(BF16)
    ng3 = norm_g.reshape(depth, 1, d)
    qg3 = q_norm_g.reshape(depth, 1, hd)
    kg3 = k_norm_g.reshape(depth, 1, hd)
    gg3 = gla_norm_g.reshape(depth, 1, dv)
    nf2 = norm_f.reshape(1, d)

    kc_sb = cache_sb_k.reshape(depth, n_pool, page, w_sb)
    vc_sb = cache_sb_v.reshape(depth, n_pool, page, w_sb)
    kc_fx = cache_fox_k.reshape(depth, n_pool, page, w_fox)
    vc_fx = cache_fox_v.reshape(depth, n_pool, page, w_fox)
    lc_fx = jnp.pad(jnp.swapaxes(cache_fox_logf, 2, 3), ((0, 0), (0, 0), (0, SUBLANES - h_fox), (0, 0)))

    m_p = batch * seq
    hp = x_prompt.reshape(m_p, d)
    hs = jnp.pad(x_sample.reshape(n_seq, d), ((0, DEC_ROWS - n_seq), (0, 0)))

    jq = c_qc * LANES // PROJ_TN
    jk = c_kc * LANES // PROJ_TN
    proj = functools.partial(_proj, tn=PROJ_TN, jq=jq, jk=jk)

    outs = [[] for _ in range(12)]
    for layer in range(depth):
        l_arr = jnp.full((1,), layer, jnp.int32)
        final = layer == depth - 1

        p = proj(l_arr, hp, ng3, w_main, qg3, kg3, tm=min(PROJ_TM, m_p))
        la, lf, fq, ft = _side(l_arr, hp, ng3, w_side, wgb, bg, bfg, batch=batch,
                               ta=min(SIDE_TA, seq), n_heads=h_fox, with_cumsum=True)
        ya = _sb_prompt(p, batch=batch, n_heads=h_sb, tb=ATT_TB, cq=c_qa, ck=c_ka, cv=c_va, cg=c_ga)
        yb, s_fin = _gla_prompt(l_arr, p, la, gg3, batch=batch, n_heads=h_gla, dk=dk, dv=dv,
                                tg=min(GLA_TG, seq), cq=c_qb * LANES // w_gk, ck=c_kb * LANES // w_gk,
                                cv=c_vb * LANES // w_gv, cg=c_gb * LANES // w_gv)
        yc = _fox_prompt(p, fq, ft, batch=batch, n_heads=h_fox, tb=ATT_TB,
                         cq=c_qc, ck=c_kc, cv=c_vc, cg=c_gc)
        hp = _out_proj(l_arr, hp, ya, yb, yc, w_o, nf2, tm=min(OUT_TM, m_p), final=final)
        col = lambda c, w: p[:, c * LANES:c * LANES + w]
        outs[0].append(col(c_ka, w_sb).reshape(batch, seq, h_sb, hd))
        outs[1].append(col(c_va, w_sb).reshape(batch, seq, h_sb, hd))
        outs[2].append(col(c_kc, w_fox).reshape(batch, seq, h_fox, hd))
        outs[3].append(col(c_vc, w_fox).reshape(batch, seq, h_fox, hd))
        outs[4].append(lf[:, SIDE_FC:SIDE_FC + h_fox].reshape(batch, seq, h_fox))
        outs[5].append(s_fin)

        ps = proj(l_arr, hs, ng3, w_main, qg3, kg3, tm=DEC_ROWS)
        las, lfs = _side(l_arr, hs, ng3, w_side, wgb, bg, bfg, batch=1, ta=DEC_ROWS,
                         n_heads=h_fox, with_cumsum=False)
        ps3 = ps.reshape(DEC_ROWS, 1, -1)
        g_pages = min(PAGES_PER_STEP, page_table.shape[1])
        yas = _sb_decode(page_table, l_arr, ps3, kc_sb, vc_sb, g_pages=g_pages, cq=c_qa, cg=c_ga)
        ybs, s_new = _gla_decode(l_arr, ps3, las.reshape(DEC_ROWS, 1, -1), state_gla, gg3,
                                 cq=c_qb, ck=c_kb, cv=c_vb, cg=c_gb)
        ycs = _fox_decode(page_table, l_arr, ps3, lfs.reshape(DEC_ROWS, 1, -1), kc_fx, vc_fx, lc_fx,
                          g_pages=g_pages, cq=c_qc, ck=c_kc, cv=c_vc, cg=c_gc)
        pad_rows = lambda y: jnp.pad(y.reshape(n_seq, -1), ((0, DEC_ROWS - n_seq), (0, 0)))
        hs = _out_proj(l_arr, hs, pad_rows(yas), pad_rows(ybs), pad_rows(ycs), w_o, nf2,
                       tm=DEC_ROWS, final=final)
        cols = lambda c, w: ps[:n_seq, c * LANES:c * LANES + w]
        outs[6].append(cols(c_ka, w_sb).reshape(n_seq, 1, h_sb, hd))
        outs[7].append(cols(c_va, w_sb).reshape(n_seq, 1, h_sb, hd))
        outs[8].append(cols(c_kc, w_fox).reshape(n_seq, 1, h_fox, hd))
        outs[9].append(cols(c_vc, w_fox).reshape(n_seq, 1, h_fox, hd))
        outs[10].append(lfs[:n_seq, SIDE_FC:SIDE_FC + h_fox].reshape(n_seq, 1, h_fox))
        outs[11].append(s_new)

    y_prompt = hp.reshape(batch, seq, d)
    y_sample = hs[:n_seq].reshape(n_seq, 1, d)
    return (y_prompt, y_sample) + tuple(jnp.stack(o) for o in outs)
```

```python
import functools

import jax
import jax.numpy as jnp
from jax import lax
from jax.experimental import pallas as pl
from jax.experimental.pallas import tpu as pltpu

F32 = jnp.float32
BF16 = jnp.bfloat16

LANES = 128
SUBLANES = 8
RMS_EPS = 1e-6
GATE_TAU = 16.0
GLA_CHUNK = 64
NEG_BIG = -1e30

SIDE_FC = 0
SIDE_AB = 8


def _dot(a, b):
    return jnp.dot(a, b, preferred_element_type=F32)


def _dot_nt(a, b):
    return lax.dot_general(a, b, (((1,), (1,)), ((), ())), preferred_element_type=F32)


def _dot_tn(a, b):
    return lax.dot_general(a, b, (((0,), (0,)), ((), ())), preferred_element_type=F32)


def _softplus(z):
    return jnp.maximum(z, 0.0) + jnp.log(1.0 + jnp.exp(-jnp.abs(z)))


def _log_sigmoid(z):
    return -_softplus(-z)


def _silu(g):
    return g / (1.0 + jnp.exp(-g))


def _rms(x, g):
    ms = jnp.mean(x * x, axis=-1, keepdims=True)
    return x * lax.rsqrt(ms + RMS_EPS) * g


def _split_bf16(x, parts):
    out = []
    r = x
    for _ in range(parts):
        p = r.astype(BF16)
        out.append(p)
        r = r - p.astype(F32)
    return out


def _dot_split(x, m, parts, left=False):
    acc = None
    for p in _split_bf16(x, parts):
        t = _dot(m, p) if left else _dot(p, m)
        acc = t if acc is None else acc + t
    return acc


def _iota(shape, dim):
    return lax.broadcasted_iota(jnp.int32, shape, dim)


def _proj_kernel(l_ref, x_ref, g_ref, w_ref, qg_ref, kg_ref, o_ref, h_ref, *, jq, jk):
    j = pl.program_id(1)

    @pl.when(j == 0)
    def _():
        h_ref[...] = _rms(x_ref[...], g_ref[...]).astype(BF16)

    acc = _dot(h_ref[...], w_ref[...])

    def headnorm(a, g):
        outs = [_rms(a[:, h * LANES:(h + 1) * LANES], g) for h in range(a.shape[1] // LANES)]
        return jnp.concatenate(outs, axis=1)

    is_q = j == jq
    is_k = j == jk

    @pl.when(is_q)
    def _():
        o_ref[...] = headnorm(acc, qg_ref[...])

    @pl.when(is_k)
    def _():
        o_ref[...] = headnorm(acc, kg_ref[...])

    @pl.when(jnp.logical_not(jnp.logical_or(is_q, is_k)))
    def _():
        o_ref[...] = acc


def _proj(l_arr, x, norm_g, w_main, qg, kg, *, tm, tn, jq, jk):
    m, d = x.shape
    n = w_main.shape[2]
    grid = (m // tm, n // tn)
    return pl.pallas_call(
        functools.partial(_proj_kernel, jq=jq, jk=jk),
        grid_spec=pltpu.PrefetchScalarGridSpec(
            num_scalar_prefetch=1,
            grid=grid,
            in_specs=[
                pl.BlockSpec((tm, d), lambda i, j, l: (i, 0)),
                pl.BlockSpec((None, 1, d), lambda i, j, l: (l[0], 0, 0)),
                pl.BlockSpec((None, d, tn), lambda i, j, l: (l[0], 0, j)),
                pl.BlockSpec((None, 1, LANES), lambda i, j, l: (l[0], 0, 0)),
                pl.BlockSpec((None, 1, LANES), lambda i, j, l: (l[0], 0, 0)),
            ],
            out_specs=pl.BlockSpec((tm, tn), lambda i, j, l: (i, j)),
            scratch_shapes=[pltpu.VMEM((tm, d), BF16)],
        ),
        out_shape=jax.ShapeDtypeStruct((m, n), F32),
        compiler_params=pltpu.CompilerParams(dimension_semantics=("arbitrary", "arbitrary")),
        name="proj",
    )(l_arr, x, norm_g, w_main, qg, kg)


def _side_kernel(l_ref, x_ref, g_ref, ws_ref, wgb_ref, bg_ref, bf_ref, la_ref, lf_ref, *rest,
                 n_heads, with_cumsum):
    x = x_ref[...]
    h = _rms(x, g_ref[...]).astype(BF16)
    s = _dot(h, ws_ref[...])
    la_ref[...] = _log_sigmoid(_dot(s.astype(BF16), wgb_ref[...]) + bg_ref[...]) / GATE_TAU
    lf = _log_sigmoid(s + bf_ref[...])
    lf_ref[...] = lf
    if with_cumsum:
        fq_ref, ft_ref, carry_ref = rest
        ta = x.shape[0]

        @pl.when(pl.program_id(1) == 0)
        def _():
            carry_ref[...] = jnp.zeros_like(carry_ref)

        lower = (_iota((ta, ta), 0) >= _iota((ta, ta), 1)).astype(BF16)
        f = _dot_split(lf, lower, 3, left=True) + carry_ref[...]
        carry_ref[...] = f[ta - 1:ta, :]
        ft = f.T
        for hh in range(n_heads):
            c = SIDE_FC + hh
            fq_ref[:, hh * LANES:(hh + 1) * LANES] = jnp.broadcast_to(f[:, c:c + 1], (ta, LANES))
            ft_ref[hh] = ft[c:c + 1, :]


def _side(l_arr, x, norm_g, w_side, wgb, bg, bfg, *, batch, ta, n_heads, with_cumsum):
    m, d = x.shape
    t = m // batch
    nt = t // ta
    n_la = wgb.shape[2]
    in_specs = [
        pl.BlockSpec((ta, d), lambda b, i, l: (b * nt + i, 0)),
        pl.BlockSpec((None, 1, d), lambda b, i, l: (l[0], 0, 0)),
        pl.BlockSpec((None, d, LANES), lambda b, i, l: (l[0], 0, 0)),
        pl.BlockSpec((None, LANES, n_la), lambda b, i, l: (l[0], 0, 0)),
        pl.BlockSpec((None, 1, n_la), lambda b, i, l: (l[0], 0, 0)),
        pl.BlockSpec((None, 1, LANES), lambda b, i, l: (l[0], 0, 0)),
    ]
    out_specs = [
        pl.BlockSpec((ta, n_la), lambda b, i, l: (b * nt + i, 0)),
        pl.BlockSpec((ta, LANES), lambda b, i, l: (b * nt + i, 0)),
    ]
    out_shape = [jax.ShapeDtypeStruct((m, n_la), F32), jax.ShapeDtypeStruct((m, LANES), F32)]
    scratch = []
    if with_cumsum:
        out_specs += [
            pl.BlockSpec((ta, n_heads * LANES), lambda b, i, l: (b * nt + i, 0)),
            pl.BlockSpec((None, n_heads, 1, ta), lambda b, i, l: (b, 0, 0, i)),
        ]
        out_shape += [jax.ShapeDtypeStruct((m, n_heads * LANES), F32),
                      jax.ShapeDtypeStruct((batch, n_heads, 1, t), F32)]
        scratch = [pltpu.VMEM((1, LANES), F32)]
    return pl.pallas_call(
        functools.partial(_side_kernel, n_heads=n_heads, with_cumsum=with_cumsum),
        grid_spec=pltpu.PrefetchScalarGridSpec(
            num_scalar_prefetch=1, grid=(batch, nt), in_specs=in_specs, out_specs=out_specs,
            scratch_shapes=scratch),
        out_shape=out_shape,
        compiler_params=pltpu.CompilerParams(dimension_semantics=("arbitrary", "arbitrary")),
        name="side",
    )(l_arr, x, norm_g, w_side, wgb, bg, bfg)


def _interleave_rows(dst_ref, src_refs, t, chunk):
    n = len(src_refs)

    def body(c, carry):
        off = pl.multiple_of(c * chunk, chunk)
        for h, src in enumerate(src_refs):
            dst_ref[pl.ds(off * n + h, chunk, stride=n), :] = src[pl.ds(off, chunk), :]
        return carry

    lax.fori_loop(0, t // chunk, body, 0)


def _attn_specs(p, *, batch, n_heads, tb, cq, ck, cv, cg):
    m = p.shape[0]
    t = m // batch
    nq = t // tb
    w = n_heads * LANES
    in_specs = [
        pl.BlockSpec((tb, w), lambda b, i: (b * nq + i, cq // n_heads)),
        pl.BlockSpec((tb, w), lambda b, i: (b * nq + i, cg // n_heads)),
    ]
    in_specs += [pl.BlockSpec((t, LANES), lambda b, i, c=ck + h: (b, c)) for h in range(n_heads)]
    in_specs += [pl.BlockSpec((t, LANES), lambda b, i, c=cv + h: (b, c)) for h in range(n_heads)]
    out_specs = [
        pl.BlockSpec((tb, w), lambda b, i: (b * nq + i, 0)),
        pl.BlockSpec((None, t * n_heads, LANES), lambda b, i: (b, 0, 0)),
        pl.BlockSpec((None, t * n_heads, LANES), lambda b, i: (b, 0, 0)),
    ]
    out_shape = [
        jax.ShapeDtypeStruct((m, w), BF16),
        jax.ShapeDtypeStruct((batch, t * n_heads, LANES), F32),
        jax.ShapeDtypeStruct((batch, t * n_heads, LANES), F32),
    ]
    return (batch, nq), in_specs, out_specs, out_shape


def _sb_prompt_kernel(q_ref, g_ref, *refs, tb, n_heads, scale):
    k_refs = refs[:n_heads]
    v_refs = refs[n_heads:2 * n_heads]
    o_ref, ko_ref, vo_ref, qs_ref, acc_ref, carry_ref = refs[2 * n_heads:]
    i = pl.program_id(1)
    t = k_refs[0].shape[0]

    @pl.when(i == 0)
    def _():
        _interleave_rows(ko_ref, k_refs, t, tb)
        _interleave_rows(vo_ref, v_refs, t, tb)

    row = _iota((tb, tb + LANES), 0)
    col = _iota((tb, tb + LANES), 1)
    later = jnp.logical_or(row > col, col >= tb).astype(BF16)
    valid = _iota((tb, tb), 1) < _iota((tb, tb), 0)
    for h in range(n_heads):
        qs_ref[h] = (q_ref[:, h * LANES:(h + 1) * LANES] * scale).astype(BF16)
    acc_ref[...] = jnp.zeros_like(acc_ref)
    carry_ref[...] = jnp.zeros_like(carry_ref)

    def block(h, j, diagonal):
        off = pl.multiple_of(j * tb, tb)
        k = k_refs[h][pl.ds(off, tb), :].astype(BF16)
        v = v_refs[h][pl.ds(off, tb), :].astype(BF16)
        z = _dot_nt(qs_ref[h], k)
        l1m = -_softplus(z)
        if diagonal:
            l1m = jnp.where(valid, l1m, 0.0)
        cum = _dot_split(l1m, later, 2)
        carry = carry_ref[h]
        w = jnp.exp(z + l1m + (cum[:, :tb] + jnp.concatenate([carry] * (tb // LANES), axis=1)))
        if diagonal:
            w = jnp.where(valid, w, 0.0)
        acc_ref[h] += _dot(w.astype(BF16), v)
        carry_ref[h] = carry + cum[:, tb:]

    for h in range(n_heads):
        block(h, i, True)

    def body(jj, c):
        for h in range(n_heads):
            block(h, i - 1 - jj, False)
        return c

    lax.fori_loop(0, i, body, 0)
    for h in range(n_heads):
        cs = slice(h * LANES, (h + 1) * LANES)
        o_ref[:, cs] = (acc_ref[h] * _silu(g_ref[:, cs])).astype(o_ref.dtype)


def _sb_prompt(p, *, batch, n_heads, tb, cq, ck, cv, cg):
    grid, in_specs, out_specs, out_shape = _attn_specs(
        p, batch=batch, n_heads=n_heads, tb=tb, cq=cq, ck=ck, cv=cv, cg=cg)
    return pl.pallas_call(
        functools.partial(_sb_prompt_kernel, tb=tb, n_heads=n_heads, scale=LANES ** -0.5),
        grid=grid,
        in_specs=in_specs,
        out_specs=out_specs,
        out_shape=out_shape,
        scratch_shapes=[
            pltpu.VMEM((n_heads, tb, LANES), BF16),
            pltpu.VMEM((n_heads, tb, LANES), F32),
            pltpu.VMEM((n_heads, tb, LANES), F32),
        ],
        compiler_params=pltpu.CompilerParams(dimension_semantics=("arbitrary", "arbitrary")),
        name="sb_prompt",
    )(*([p] * (2 + 2 * n_heads)))


def _fox_prompt_kernel(q_ref, g_ref, *refs, tb, n_heads, scale):
    k_refs = refs[:n_heads]
    v_refs = refs[n_heads:2 * n_heads]
    fq_ref, ft_ref, o_ref, ko_ref, vo_ref, qs_ref, acc_ref, m_ref, l_ref = refs[2 * n_heads:]
    i = pl.program_id(1)
    t = k_refs[0].shape[0]

    @pl.when(i == 0)
    def _():
        _interleave_rows(ko_ref, k_refs, t, tb)
        _interleave_rows(vo_ref, v_refs, t, tb)

    valid = _iota((tb, tb), 1) <= _iota((tb, tb), 0)
    ones = jnp.ones((tb, LANES), BF16)
    wide = lambda a: jnp.concatenate([a] * (tb // LANES), axis=1)
    for h in range(n_heads):
        qs_ref[h] = (q_ref[:, h * LANES:(h + 1) * LANES] * scale).astype(BF16)
    acc_ref[...] = jnp.zeros_like(acc_ref)
    l_ref[...] = jnp.zeros_like(l_ref)
    m_ref[...] = jnp.full_like(m_ref, NEG_BIG)

    def block(h, j, diagonal):
        off = pl.multiple_of(j * tb, tb)
        k = k_refs[h][pl.ds(off, tb), :].astype(BF16)
        v1 = jnp.concatenate([v_refs[h][pl.ds(off, tb), :].astype(BF16), ones], axis=1)
        fq = fq_ref[:, h * LANES:(h + 1) * LANES]
        fk = ft_ref[h, :, pl.ds(off, tb)]
        z = _dot_nt(qs_ref[h], k) + (wide(fq) - fk)
        if diagonal:
            z = jnp.where(valid, z, NEG_BIG)
        m_old = m_ref[h]
        m_new = jnp.maximum(m_old, jnp.max(z, axis=1, keepdims=True))
        alpha = jnp.exp(m_old - m_new)
        pv = _dot(jnp.exp(z - wide(m_new)).astype(BF16), v1)
        acc_ref[h] = alpha * acc_ref[h] + pv[:, :LANES]
        l_ref[h] = alpha * l_ref[h] + pv[:, LANES:]
        m_ref[h] = m_new

    for h in range(n_heads):
        block(h, i, True)

    def body(jj, c):
        for h in range(n_heads):
            block(h, jj, False)
        return c

    lax.fori_loop(0, i, body, 0)
    for h in range(n_heads):
        cs = slice(h * LANES, (h + 1) * LANES)
        o_ref[:, cs] = ((acc_ref[h] / l_ref[h]) * _silu(g_ref[:, cs])).astype(o_ref.dtype)


def _fox_prompt(p, fq, ft, *, batch, n_heads, tb, cq, ck, cv, cg):
    grid, in_specs, out_specs, out_shape = _attn_specs(
        p, batch=batch, n_heads=n_heads, tb=tb, cq=cq, ck=ck, cv=cv, cg=cg)
    m = p.shape[0]
    t = m // batch
    nq = t // tb
    in_specs += [
        pl.BlockSpec((tb, n_heads * LANES), lambda b, i: (b * nq + i, 0)),
        pl.BlockSpec((None, n_heads, 1, t), lambda b, i: (b, 0, 0, 0)),
    ]
    return pl.pallas_call(
        functools.partial(_fox_prompt_kernel, tb=tb, n_heads=n_heads, scale=LANES ** -0.5),
        grid=grid,
        in_specs=in_specs,
        out_specs=out_specs,
        out_shape=out_shape,
        scratch_shapes=[
            pltpu.VMEM((n_heads, tb, LANES), BF16),
            pltpu.VMEM((n_heads, tb, LANES), F32),
            pltpu.VMEM((n_heads, tb, LANES), F32),
            pltpu.VMEM((n_heads, tb, LANES), F32),
        ],
        compiler_params=pltpu.CompilerParams(dimension_semantics=("arbitrary", "arbitrary")),
        name="fox_prompt",
    )(*([p] * (2 + 2 * n_heads)), fq, ft)


def _gla_prompt_kernel(l_ref, q_ref, k_ref, v_ref, g_ref, la_ref, ng_ref, y_ref, s_ref, st_ref, *,
                       tg, n_pairs, scale):
    c_len = GLA_CHUNK
    half = LANES // 2
    ti = pl.program_id(1)

    @pl.when(ti == 0)
    def _():
        st_ref[...] = jnp.zeros_like(st_ref)

    lower = (_iota((c_len, c_len), 0) >= _iota((c_len, c_len), 1)).astype(BF16)
    lane = _iota((c_len, LANES), 1)
    first = lane < half
    r2 = _iota((2 * c_len, 2 * c_len), 0)
    c2 = _iota((2 * c_len, 2 * c_len), 1)
    sh = c_len.bit_length() - 1
    att_mask = jnp.logical_and((r2 >> sh) == (c2 >> sh), (c2 & (c_len - 1)) <= (r2 & (c_len - 1)))
    lane_sq = _iota((LANES, LANES), 1) < half
    ng = ng_ref[...]

    for c in range(tg // c_len):
        rows = slice(c * c_len, (c + 1) * c_len)
        la = la_ref[rows, :]
        b = _dot_split(la, lower, 3, left=True)
        b_last = b[c_len - 1:c_len, :]
        qd = q_ref[rows, :] * scale * jnp.exp(b)
        kk = k_ref[rows, :]
        kd = kk * jnp.exp(-b)
        ku = kk * jnp.exp(b_last - b)
        e_last = jnp.exp(b_last)
        for pr in range(n_pairs):
            ls = slice(pr * LANES, (pr + 1) * LANES)
            qd_p = qd[:, ls]
            qs = jnp.concatenate([jnp.where(first, qd_p, 0.0), jnp.where(first, 0.0, qd_p)],
                                 axis=0).astype(BF16)
            kd_p = kd[:, ls].astype(BF16)
            kd2 = jnp.concatenate([kd_p, kd_p], axis=0)
            att = jnp.where(att_mask, _dot_nt(qs, kd2), 0.0).astype(BF16)
            v_pair = v_ref[rows, 2 * pr * LANES:(2 * pr + 2) * LANES].astype(BF16)
            v_stack = jnp.concatenate([v_pair[:, :LANES], v_pair[:, LANES:]], axis=0)
            st = st_ref[pr]
            o = _dot(att, v_stack) + _dot_nt(qs, st.astype(BF16))
            upd = _dot_tn(v_pair, ku[:, ls].astype(BF16))
            st_ref[pr] = st * e_last[:, ls] + jnp.where(lane_sq, upd[:LANES], upd[LANES:])
            for s in range(2):
                hh = 2 * pr + s
                cs = slice(hh * LANES, (hh + 1) * LANES)
                oh = _rms(o[s * c_len:(s + 1) * c_len], ng)
                y_ref[rows, cs] = (oh * _silu(g_ref[rows, cs])).astype(y_ref.dtype)

    @pl.when(ti == pl.num_programs(1) - 1)
    def _():
        for pr in range(n_pairs):
            s_t = st_ref[pr].T
            s_ref[2 * pr] = s_t[:half]
            s_ref[2 * pr + 1] = s_t[half:]


def _gla_prompt(l_arr, p, la, ng, *, batch, n_heads, dk, dv, tg, cq, ck, cv, cg):
    m = p.shape[0]
    t = m // batch
    nt = t // tg
    wk = n_heads * dk
    wv = n_heads * dv
    n_pairs = n_heads // 2
    return pl.pallas_call(
        functools.partial(_gla_prompt_kernel, tg=tg, n_pairs=n_pairs, scale=dk ** -0.5),
        grid_spec=pltpu.PrefetchScalarGridSpec(
            num_scalar_prefetch=1,
            grid=(batch, nt),
            in_specs=[
                pl.BlockSpec((tg, wk), lambda b, i, l: (b * nt + i, cq)),
                pl.BlockSpec((tg, wk), lambda b, i, l: (b * nt + i, ck)),
                pl.BlockSpec((tg, wv), lambda b, i, l: (b * nt + i, cv)),
                pl.BlockSpec((tg, wv), lambda b, i, l: (b * nt + i, cg)),
                pl.BlockSpec((tg, wk), lambda b, i, l: (b * nt + i, 0)),
                pl.BlockSpec((None, 1, dv), lambda b, i, l: (l[0], 0, 0)),
            ],
            out_specs=[
                pl.BlockSpec((tg, wv), lambda b, i, l: (b * nt + i, 0)),
                pl.BlockSpec((None, n_heads, dk, dv), lambda b, i, l: (b, 0, 0, 0)),
            ],
            scratch_shapes=[pltpu.VMEM((n_pairs, LANES, LANES), F32)],
        ),
        out_shape=[jax.ShapeDtypeStruct((m, wv), BF16),
                   jax.ShapeDtypeStruct((batch, n_heads, dk, dv), F32)],
        compiler_params=pltpu.CompilerParams(dimension_semantics=("arbitrary", "arbitrary")),
        name="gla_prompt",
    )(l_arr, p, p, p, p, la, ng)


def _out_kernel(l_ref, x_ref, ya_ref, yb_ref, yc_ref, w0, w1, w2, w3, nf_ref, o_ref, *, final):
    wq = w0.shape[0]
    yb = yb_ref[...].astype(BF16)
    acc = x_ref[...] + _dot(ya_ref[...].astype(BF16), w0[...])
    acc = acc + _dot(yb[:, :wq], w1[...]) + _dot(yb[:, wq:], w2[...])
    acc = acc + _dot(yc_ref[...].astype(BF16), w3[...])
    o_ref[...] = _rms(acc, nf_ref[...]) if final else acc


def _out_proj(l_arr, x, ya, yb, yc, w_out, nf, *, tm, final):
    m, d = x.shape
    wq = ya.shape[1]
    assert yb.shape[1] == 2 * wq and yc.shape[1] == wq and w_out.shape[1] == 4 * wq
    wspec = lambda r: pl.BlockSpec((None, wq, d), lambda i, l, r=r: (l[0], r, 0))
    return pl.pallas_call(
        functools.partial(_out_kernel, final=final),
        grid_spec=pltpu.PrefetchScalarGridSpec(
            num_scalar_prefetch=1,
            grid=(m // tm,),
            in_specs=[
                pl.BlockSpec((tm, d), lambda i, l: (i, 0)),
                pl.BlockSpec((tm, wq), lambda i, l: (i, 0)),
                pl.BlockSpec((tm, 2 * wq), lambda i, l: (i, 0)),
                pl.BlockSpec((tm, wq), lambda i, l: (i, 0)),
                wspec(0), wspec(1), wspec(2), wspec(3),
                pl.BlockSpec((1, d), lambda i, l: (0, 0)),
            ],
            out_specs=pl.BlockSpec((tm, d), lambda i, l: (i, 0)),
        ),
        out_shape=jax.ShapeDtypeStruct((m, d), F32),
        compiler_params=pltpu.CompilerParams(dimension_semantics=("arbitrary",)),
        name="out_proj",
    )(l_arr, x, ya, yb, yc, w_out, w_out, w_out, w_out, nf)


def _rev_excl_cumsum(x, later, parts):
    r, n = x.shape
    nb = n // LANES
    xs = jnp.concatenate([x[:, i * LANES:(i + 1) * LANES] for i in range(nb)], axis=0)
    cs = _dot_split(xs, later, parts)
    tot = jnp.sum(xs, axis=1, keepdims=True)
    run = jnp.zeros((r, 1), F32)
    blocks = [None] * nb
    for i in range(nb - 1, -1, -1):
        blocks[i] = cs[i * r:(i + 1) * r] + run
        run = run + tot[i * r:(i + 1) * r]
    return jnp.concatenate(blocks, axis=1), run


def _head_rows(row_vec, h):
    part = jnp.broadcast_to(row_vec[:, h * LANES:(h + 1) * LANES], (SUBLANES, LANES))
    return jnp.where(_iota((SUBLANES, LANES), 0) == h, part, 0.0)


def _head_block(buf, slot, h, tok, n_heads):
    return buf[slot, pl.ds(h, tok, stride=n_heads), :].astype(BF16)


def _paged_scores(q_row, kbuf, slot, tok, n_heads):
    z = None
    for h in range(n_heads):
        t = _dot_nt(_head_rows(q_row, h).astype(BF16), _head_block(kbuf, slot, h, tok, n_heads))
        z = t if z is None else z + t
    return z


def _own_rows(acc_ref, n_heads, denom=None):
    parts = []
    for h in range(n_heads):
        a = acc_ref[h] if denom is None else acc_ref[h] / denom
        parts.append(a[h:h + 1, :])
    return jnp.concatenate(parts, axis=1)


def _page_copies(pt_ref, l, bb, gg, slot, srcs, bufs, sem, *, n_groups, g_pages):
    base = (n_groups - 1 - gg) * g_pages
    out = []
    for r in range(g_pages):
        pg = pt_ref[bb, base + r]
        for si, (src, buf) in enumerate(zip(srcs, bufs)):
            rows = src.shape[2]
            out.append(pltpu.make_async_copy(
                src.at[l, pg], buf.at[slot, pl.ds(r * rows, rows)], sem.at[si, slot]))
    return out


def _paged_pipeline(pt_ref, l, srcs, bufs, sem, *, n_groups, g_pages):
    bb = pl.program_id(0)
    gg = pl.program_id(1)
    step = bb * n_groups + gg
    total = pl.num_programs(0) * n_groups
    slot = lax.rem(step, 2)
    mk = functools.partial(_page_copies, pt_ref, l, srcs=srcs, bufs=bufs, sem=sem,
                           n_groups=n_groups, g_pages=g_pages)

    @pl.when(step == 0)
    def _():
        for cp in mk(bb, gg, slot):
            cp.start()

    @pl.when(step + 1 < total)
    def _():
        wrap = gg + 1 == n_groups
        nb = jnp.where(wrap, bb + 1, bb)
        ng = jnp.where(wrap, 0, gg + 1)
        for cp in mk(nb, ng, 1 - slot):
            cp.start()

    for cp in mk(bb, gg, slot):
        cp.wait()
    return slot


def _sb_decode_kernel(pt_ref, l_ref, p_ref, kc_ref, vc_ref, o_ref, kbuf, vbuf, sem, acc_ref,
                      carry_ref, *, n_groups, g_pages, tok, n_heads, scale, cq, cg):
    gg = pl.program_id(1)
    w = n_heads * LANES
    slot = _paged_pipeline(pt_ref, l_ref[0], (kc_ref, vc_ref), (kbuf, vbuf), sem,
                           n_groups=n_groups, g_pages=g_pages)

    @pl.when(gg == 0)
    def _():
        acc_ref[...] = jnp.zeros_like(acc_ref)
        carry_ref[...] = jnp.zeros_like(carry_ref)

    q_row = p_ref[:, cq * LANES:cq * LANES + w] * scale
    later = (_iota((LANES, LANES), 0) > _iota((LANES, LANES), 1)).astype(BF16)
    z = _paged_scores(q_row, kbuf, slot, tok, n_heads)
    l1m = -_softplus(z)
    cum, tot = _rev_excl_cumsum(l1m, later, 2)
    wgt = jnp.exp(z + l1m + cum + carry_ref[...]).astype(BF16)
    for h in range(n_heads):
        acc_ref[h] += _dot(wgt, _head_block(vbuf, slot, h, tok, n_heads))
    carry_ref[...] += tot

    @pl.when(gg == n_groups - 1)
    def _():
        gate = p_ref[:, cg * LANES:cg * LANES + w]
        o_ref[...] = _own_rows(acc_ref, n_heads) * _silu(gate)


def _sb_decode(page_table, l_arr, p, kc, vc, *, n_heads, g_pages, cq, cg):
    n_seq, n_pages = page_table.shape
    rows = kc.shape[2]
    w = n_heads * LANES
    n_groups = n_pages // g_pages
    tok = g_pages * rows // n_heads
    return pl.pallas_call(
        functools.partial(_sb_decode_kernel, n_groups=n_groups, g_pages=g_pages, tok=tok,
                          n_heads=n_heads, scale=LANES ** -0.5, cq=cq, cg=cg),
        grid_spec=pltpu.PrefetchScalarGridSpec(
            num_scalar_prefetch=2,
            grid=(n_seq, n_groups),
            in_specs=[
                pl.BlockSpec((None, 1, p.shape[2]), lambda b, g, pt, l: (b, 0, 0)),
                pl.BlockSpec(memory_space=pl.ANY),
                pl.BlockSpec(memory_space=pl.ANY),
            ],
            out_specs=pl.BlockSpec((None, 1, w), lambda b, g, pt, l: (b, 0, 0)),
            scratch_shapes=[
                pltpu.VMEM((2, g_pages * rows, LANES), F32),
                pltpu.VMEM((2, g_pages * rows, LANES), F32),
                pltpu.SemaphoreType.DMA((2, 2)),
                pltpu.VMEM((n_heads, SUBLANES, LANES), F32),
                pltpu.VMEM((SUBLANES, 1), F32),
            ],
        ),
        out_shape=jax.ShapeDtypeStruct((n_seq, 1, w), F32),
        compiler_params=pltpu.CompilerParams(dimension_semantics=("arbitrary", "arbitrary")),
        name="sb_decode",
    )(page_table, l_arr, p, kc, vc)


def _fox_decode_kernel(pt_ref, l_ref, p_ref, lfn_ref, kc_ref, vc_ref, lc_ref, o_ref, kbuf, vbuf,
                       lbuf, sem, acc_ref, m_ref, l_sum_ref, carry_ref, *, n_groups, g_pages, tok,
                       n_heads, scale, cq, ck, cv, cg):
    gg = pl.program_id(1)
    w = n_heads * LANES
    slot = _paged_pipeline(pt_ref, l_ref[0], (kc_ref, vc_ref, lc_ref), (kbuf, vbuf, lbuf), sem,
                           n_groups=n_groups, g_pages=g_pages)
    q_row = p_ref[:, cq * LANES:cq * LANES + w] * scale

    @pl.when(gg == 0)
    def _():
        k_new = p_ref[:, ck * LANES:ck * LANES + w]
        v_new = p_ref[:, cv * LANES:cv * LANES + w]
        z_new = jnp.zeros((SUBLANES, 1), F32)
        for h in range(n_heads):
            cs = slice(h * LANES, (h + 1) * LANES)
            z_new = z_new + jnp.sum(_head_rows(q_row, h) * k_new[:, cs], axis=1, keepdims=True)
            acc_ref[h] = jnp.broadcast_to(v_new[:, cs], (SUBLANES, LANES))
        m_ref[...] = z_new
        l_sum_ref[...] = jnp.ones_like(l_sum_ref)
        pick = _iota((SUBLANES, LANES), 1) == _iota((SUBLANES, LANES), 0) + SIDE_FC
        lf_row = jnp.broadcast_to(lfn_ref[...], (SUBLANES, LANES))
        carry_ref[...] = jnp.sum(jnp.where(pick, lf_row, 0.0), axis=1, keepdims=True)

    later = (_iota((LANES, LANES), 0) > _iota((LANES, LANES), 1)).astype(BF16)
    lf = jnp.concatenate([lbuf[slot, pl.ds(r * SUBLANES, SUBLANES), :] for r in range(g_pages)],
                         axis=1)
    bias, tot = _rev_excl_cumsum(lf, later, 3)
    z = _paged_scores(q_row, kbuf, slot, tok, n_heads) + (bias + carry_ref[...])
    m_old = m_ref[...]
    m_new = jnp.maximum(m_old, jnp.max(z, axis=1, keepdims=True))
    alpha = jnp.exp(m_old - m_new)
    pr = jnp.exp(z - m_new)
    l_sum_ref[...] = alpha * l_sum_ref[...] + jnp.sum(pr, axis=1, keepdims=True)
    pr = pr.astype(BF16)
    for h in range(n_heads):
        acc_ref[h] = alpha * acc_ref[h] + _dot(pr, _head_block(vbuf, slot, h, tok, n_heads))
    m_ref[...] = m_new
    carry_ref[...] += tot

    @pl.when(gg == n_groups - 1)
    def _():
        gate = p_ref[:, cg * LANES:cg * LANES + w]
        o_ref[...] = _own_rows(acc_ref, n_heads, l_sum_ref[...]) * _silu(gate)


def _fox_decode(page_table, l_arr, p, lf_new, kc, vc, lc, *, n_heads, g_pages, cq, ck, cv, cg):
    n_seq, n_pages = page_table.shape
    rows = kc.shape[2]
    w = n_heads * LANES
    n_groups = n_pages // g_pages
    tok = g_pages * rows // n_heads
    return pl.pallas_call(
        functools.partial(_fox_decode_kernel, n_groups=n_groups, g_pages=g_pages, tok=tok,
                          n_heads=n_heads, scale=LANES ** -0.5, cq=cq, ck=ck, cv=cv, cg=cg),
        grid_spec=pltpu.PrefetchScalarGridSpec(
            num_scalar_prefetch=2,
            grid=(n_seq, n_groups),
            in_specs=[
                pl.BlockSpec((None, 1, p.shape[2]), lambda b, g, pt, l: (b, 0, 0)),
                pl.BlockSpec((None, 1, lf_new.shape[2]), lambda b, g, pt, l: (b, 0, 0)),
                pl.BlockSpec(memory_space=pl.ANY),
                pl.BlockSpec(memory_space=pl.ANY),
                pl.BlockSpec(memory_space=pl.ANY),
            ],
            out_specs=pl.BlockSpec((None, 1, w), lambda b, g, pt, l: (b, 0, 0)),
            scratch_shapes=[
                pltpu.VMEM((2, g_pages * rows, LANES), F32),
                pltpu.VMEM((2, g_pages * rows, LANES), F32),
                pltpu.VMEM((2, g_pages * SUBLANES, LANES), F32),
                pltpu.SemaphoreType.DMA((3, 2)),
                pltpu.VMEM((n_heads, SUBLANES, LANES), F32),
                pltpu.VMEM((SUBLANES, 1), F32),
                pltpu.VMEM((SUBLANES, 1), F32),
                pltpu.VMEM((SUBLANES, 1), F32),
            ],
        ),
        out_shape=jax.ShapeDtypeStruct((n_seq, 1, w), F32),
        compiler_params=pltpu.CompilerParams(dimension_semantics=("arbitrary", "arbitrary")),
        name="fox_decode",
    )(page_table, l_arr, p, lf_new, kc, vc, lc)


def _gla_decode_kernel(l_ref, p_ref, la_ref, s0_ref, ng_ref, y_ref, s_ref, *, n_heads, dk, dv,
                       scale, cq, ck, cv, cg):
    bb = pl.program_id(0)
    wk = n_heads * dk

    q_row = p_ref[:,cq * LANES:cq * LANES + wk]
    k_row = p_ref[:,ck * LANES:ck * LANES + wk]
    e_row = jnp.exp(la_ref[...])
    lane = _iota((dk, wk), 1)
    sub = _iota((dk, wk), 0)
    ng = ng_ref[...]

    def column(row_vec, hh):
        pick = lane == sub + hh * dk
        return jnp.sum(jnp.where(pick, jnp.broadcast_to(row_vec, (dk, wk)), 0.0),
                       axis=1, keepdims=True)

    for hh in range(n_heads):
        cs = slice(hh * dv, (hh + 1) * dv)
        v_row = p_ref[:,cv * LANES + hh * dv:cv * LANES + (hh + 1) * dv]
        s_new = column(e_row, hh) * s0_ref[hh] + column(k_row, hh) * v_row
        s_ref[hh] = s_new
        o = jnp.sum((column(q_row, hh) * scale) * s_new, axis=0, keepdims=True)
        gate = p_ref[:,cg * LANES + hh * dv:cg * LANES + (hh + 1) * dv]
        y_ref[:, cs] = _rms(o, ng) * _silu(gate)


def _gla_decode(l_arr, p, la, state, ng, *, cq, ck, cv, cg):
    _, n_seq, n_heads, dk, dv = state.shape
    return pl.pallas_call(
        functools.partial(_gla_decode_kernel, n_heads=n_heads, dk=dk, dv=dv, scale=dk ** -0.5,
                          cq=cq, ck=ck, cv=cv, cg=cg),
        grid_spec=pltpu.PrefetchScalarGridSpec(
            num_scalar_prefetch=1,
            grid=(n_seq,),
            in_specs=[
                pl.BlockSpec((None, 1, p.shape[2]), lambda b, l: (b, 0, 0)),
                pl.BlockSpec((None, 1, la.shape[2]), lambda b, l: (b, 0, 0)),
                pl.BlockSpec((None, None, n_heads, dk, dv), lambda b, l: (l[0], b, 0, 0, 0)),
                pl.BlockSpec((None, 1, dv), lambda b, l: (l[0], 0, 0)),
            ],
            out_specs=[
                pl.BlockSpec((None, 1, n_heads * dv), lambda b, l: (b, 0, 0)),
                pl.BlockSpec((None, n_heads, dk, dv), lambda b, l: (b, 0, 0, 0)),
            ],
        ),
        out_shape=[jax.ShapeDtypeStruct((n_seq, 1, n_heads * dv), F32),
                   jax.ShapeDtypeStruct((n_seq, n_heads, dk, dv), F32)],
        compiler_params=pltpu.CompilerParams(dimension_semantics=("arbitrary",)),
        name="gla_decode",
    )(l_arr, p, la, state, ng)


DEC_ROWS = 16
PROJ_TM = 1024
PROJ_TN = 512
SIDE_TA = 256
ATT_TB = 256
GLA_TG = 256
OUT_TM = 512
PAGES_PER_STEP = 8


def kernel(x_prompt, x_sample, cache_sb_k, cache_sb_v, cache_fox_k, cache_fox_v, cache_fox_logf,
           state_gla, page_table, norm_g, w_in, w_gate_b, b_gate, b_forget, q_norm_g, k_norm_g,
           gla_norm_g, w_out, norm_f):
    batch, seq, d = x_prompt.shape
    n_seq = x_sample.shape[0]
    depth, n_pool, page, h_sb, hd = cache_sb_k.shape
    h_fox = cache_fox_k.shape[3]
    _, _, h_gla, dk, dv = state_gla.shape
    rank = w_gate_b.shape[1]
    assert hd == LANES and dv == LANES and 2 * dk == LANES and x_sample.shape[1] == 1
    w_sb, w_fox, w_gk, w_gv = h_sb * hd, h_fox * hd, h_gla * dk, h_gla * dv
    assert w_sb == w_fox == w_gk and w_gv == 2 * w_sb

    sizes = (w_sb,) * 4 + (w_gk, w_gk, w_gv, w_gv, rank) + (w_fox,) * 4 + (h_fox,)
    offs = [0]
    for s in sizes:
        offs.append(offs[-1] + s)
    assert offs[-1] == w_in.shape[2]
    o_ab, o_qc, o_fc = offs[8], offs[9], offs[13]
    n_main = o_ab + (o_fc - o_qc)
    c_qa, c_ka, c_va, c_ga = (offs[i] // LANES for i in range(4))
    c_qb, c_kb, c_vb, c_gb = (offs[i] // LANES for i in range(4, 8))
    c_qc, c_kc, c_vc, c_gc = ((offs[i] - rank) // LANES for i in range(9, 13))

    w_main = jnp.concatenate([w_in[:, :, :o_ab], w_in[:, :, o_qc:o_fc]], axis=2).astype(BF16)
    w_side = jnp.zeros((depth, d, LANES), F32)
    w_side = w_side.at[:, :, SIDE_FC:SIDE_FC + h_fox].set(w_in[:, :, o_fc:])
    w_side = w_side.at[:, :, SIDE_AB:SIDE_AB + rank].set(w_in[:, :, o_ab:o_qc]).astype(BF16)
    wgb = jnp.zeros((depth, LANES, w_gk), F32).at[:, SIDE_AB:SIDE_AB + rank].set(w_gate_b)
    wgb = wgb.astype(BF16)
    bg = b_gate.reshape(depth, 1, w_gk)
    bfg = jnp.zeros((depth, 1, LANES), F32).at[:, 0, SIDE_FC:SIDE_FC + h_fox].set(b_forget)
    w_o = w_out.astype(BF16)
    ng3 = norm_g.reshape(depth, 1, d)
    qg3 = q_norm_g.reshape(depth, 1, hd)
    kg3 = k_norm_g.reshape(depth, 1, hd)
    gg3 = gla_norm_g.reshape(depth, 1, dv)
    nf2 = norm_f.reshape(1, d)

    kc_sb = cache_sb_k.reshape(depth, n_pool, page * h_sb, hd)
    vc_sb = cache_sb_v.reshape(depth, n_pool, page * h_sb, hd)
    kc_fx = cache_fox_k.reshape(depth, n_pool, page * h_fox, hd)
    vc_fx = cache_fox_v.reshape(depth, n_pool, page * h_fox, hd)
    lc_fx = jnp.pad(jnp.swapaxes(cache_fox_logf, 2, 3), ((0, 0), (0, 0), (0, SUBLANES - h_fox), (0, 0)))

    m_p = batch * seq
    hp = x_prompt.reshape(m_p, d)
    hs = jnp.pad(x_sample.reshape(n_seq, d), ((0, DEC_ROWS - n_seq), (0, 0)))

    jq = c_qc * LANES // PROJ_TN
    jk = c_kc * LANES // PROJ_TN
    proj = functools.partial(_proj, tn=PROJ_TN, jq=jq, jk=jk)

    outs = [[] for _ in range(12)]
    for layer in range(depth):
        l_arr = jnp.full((1,), layer, jnp.int32)
        final = layer == depth - 1

        p = proj(l_arr, hp, ng3, w_main, qg3, kg3, tm=min(PROJ_TM, m_p))
        la, lf, fq, ft = _side(l_arr, hp, ng3, w_side, wgb, bg, bfg, batch=batch,
                               ta=min(SIDE_TA, seq), n_heads=h_fox, with_cumsum=True)
        tb = min(ATT_TB, seq)
        ya, ka_o, va_o = _sb_prompt(p, batch=batch, n_heads=h_sb, tb=tb,
                                    cq=c_qa, ck=c_ka, cv=c_va, cg=c_ga)
        yb, s_fin = _gla_prompt(l_arr, p, la, gg3, batch=batch, n_heads=h_gla, dk=dk, dv=dv,
                                tg=min(GLA_TG, seq), cq=c_qb * LANES // w_gk, ck=c_kb * LANES // w_gk,
                                cv=c_vb * LANES // w_gv, cg=c_gb * LANES // w_gv)
        yc, kc_o, vc_o = _fox_prompt(p, fq, ft, batch=batch, n_heads=h_fox, tb=tb,
                         cq=c_qc, ck=c_kc, cv=c_vc, cg=c_gc)
        hp = _out_proj(l_arr, hp, ya, yb, yc, w_o, nf2, tm=min(OUT_TM, m_p), final=final)
        outs[0].append(ka_o.reshape(batch, seq, h_sb, hd))
        outs[1].append(va_o.reshape(batch, seq, h_sb, hd))
        outs[2].append(kc_o.reshape(batch, seq, h_fox, hd))
        outs[3].append(vc_o.reshape(batch, seq, h_fox, hd))
        outs[4].append(lf[:, SIDE_FC:SIDE_FC + h_fox].reshape(batch, seq, h_fox))
        outs[5].append(s_fin)

        ps = proj(l_arr, hs, ng3, w_main, qg3, kg3, tm=DEC_ROWS)
        las, lfs = _side(l_arr, hs, ng3, w_side, wgb, bg, bfg, batch=1, ta=DEC_ROWS,
                         n_heads=h_fox, with_cumsum=False)
        ps3 = ps.reshape(DEC_ROWS, 1, -1)
        g_pages = min(PAGES_PER_STEP, page_table.shape[1])
        yas = _sb_decode(page_table, l_arr, ps3, kc_sb, vc_sb, n_heads=h_sb, g_pages=g_pages,
                         cq=c_qa, cg=c_ga)
        ybs, s_new = _gla_decode(l_arr, ps3, las.reshape(DEC_ROWS, 1, -1), state_gla, gg3,
                                 cq=c_qb, ck=c_kb, cv=c_vb, cg=c_gb)
        ycs = _fox_decode(page_table, l_arr, ps3, lfs.reshape(DEC_ROWS, 1, -1), kc_fx, vc_fx, lc_fx,
                          n_heads=h_fox, g_pages=g_pages, cq=c_qc, ck=c_kc, cv=c_vc, cg=c_gc)
        pad_rows = lambda y: jnp.pad(y.reshape(n_seq, -1), ((0, DEC_ROWS - n_seq), (0, 0)))
        hs = _out_proj(l_arr, hs, pad_rows(yas), pad_rows(ybs), pad_rows(ycs), w_o, nf2,
                       tm=DEC_ROWS, final=final)
        cols = lambda c, w: ps[:n_seq, c * LANES:c * LANES + w]
        outs[6].append(cols(c_ka, w_sb).reshape(n_seq, 1, h_sb, hd))
        outs[7].append(cols(c_va, w_sb).reshape(n_seq, 1, h_sb, hd))
        outs[8].append(cols(c_kc, w_fox).reshape(n_seq, 1, h_fox, hd))
        outs[9].append(cols(c_vc, w_fox).reshape(n_seq, 1, h_fox, hd))
        outs[10].append(lfs[:n_seq, SIDE_FC:SIDE_FC + h_fox].reshape(n_seq, 1, h_fox))
        outs[11].append(s_new)

    y_prompt = hp.reshape(batch, seq, d)
    y_sample = hs[:n_seq].reshape(n_seq, 1, d)
    return (y_prompt, y_sample) + tuple(jnp.stack(o) for o in outs)
```

```python
import functools

import jax
import jax.numpy as jnp
from jax import lax
from jax.experimental import pallas as pl
from jax.experimental.pallas import tpu as pltpu

F32 = jnp.float32
BF16 = jnp.bfloat16

LANES = 128
SUBLANES = 8
RMS_EPS = 1e-6
GATE_TAU = 16.0
GLA_CHUNK = 64
NEG_BIG = -1e30

SIDE_FC = 0
SIDE_AB = 8


def _dot(a, b):
    return jnp.dot(a, b, preferred_element_type=F32)


def _dot_nt(a, b):
    return lax.dot_general(a, b, (((1,), (1,)), ((), ())), preferred_element_type=F32)


def _dot_tn(a, b):
    return lax.dot_general(a, b, (((0,), (0,)), ((), ())), preferred_element_type=F32)


def _softplus(z):
    return jnp.maximum(z, 0.0) + jnp.log(1.0 + jnp.exp(-jnp.abs(z)))


def _log_sigmoid(z):
    return -_softplus(-z)


def _silu(g):
    return g / (1.0 + jnp.exp(-g))


def _rms(x, g):
    ms = jnp.mean(x * x, axis=-1, keepdims=True)
    return x * lax.rsqrt(ms + RMS_EPS) * g


def _split_bf16(x, parts):
    out = []
    r = x
    for _ in range(parts):
        p = r.astype(BF16)
        out.append(p)
        r = r - p.astype(F32)
    return out


def _dot_split(x, m, parts, left=False):
    acc = None
    for p in _split_bf16(x, parts):
        t = _dot(m, p) if left else _dot(p, m)
        acc = t if acc is None else acc + t
    return acc


def _iota(shape, dim):
    return lax.broadcasted_iota(jnp.int32, shape, dim)


def _norm_kernel(l_ref, x_ref, g_ref, h_ref):
    h_ref[...] = _rms(x_ref[...], g_ref[...]).astype(BF16)


def _norm(l_arr, x, norm_g, *, tm):
    m, d = x.shape
    return pl.pallas_call(
        _norm_kernel,
        grid_spec=pltpu.PrefetchScalarGridSpec(
            num_scalar_prefetch=1,
            grid=(m // tm,),
            in_specs=[
                pl.BlockSpec((tm, d), lambda i, l: (i, 0)),
                pl.BlockSpec((None, 1, d), lambda i, l: (l[0], 0, 0)),
            ],
            out_specs=pl.BlockSpec((tm, d), lambda i, l: (i, 0)),
        ),
        out_shape=jax.ShapeDtypeStruct((m, d), BF16),
        compiler_params=pltpu.CompilerParams(dimension_semantics=("arbitrary",)),
        name="norm",
    )(l_arr, x, norm_g)


def _proj_kernel(l_ref, h_ref, wa_ref, wb_ref, qg_ref, kg_ref, o_ref, *, ja, jn):
    j = pl.program_id(1)

    @pl.when(j < ja)
    def _():
        o_ref[...] = _dot(h_ref[...], wa_ref[...])

    @pl.when(jnp.logical_and(j >= ja, j != jn))
    def _():
        o_ref[...] = _dot(h_ref[...], wb_ref[...])

    @pl.when(j == jn)
    def _():
        acc = _dot(h_ref[...], wb_ref[...])
        n_h = acc.shape[1] // LANES
        for hh in range(n_h):
            cs = slice(hh * LANES, (hh + 1) * LANES)
            g = qg_ref[...] if hh < n_h // 2 else kg_ref[...]
            o_ref[:, cs] = _rms(acc[:, cs], g)


def _proj(l_arr, h, w_a, w_b, qg, kg, *, tm, tn):
    m, d = h.shape
    ja = w_a.shape[2] // tn
    jb = w_b.shape[2] // tn
    jn = ja
    return pl.pallas_call(
        functools.partial(_proj_kernel, ja=ja, jn=jn),
        grid_spec=pltpu.PrefetchScalarGridSpec(
            num_scalar_prefetch=1,
            grid=(m // tm, ja + jb),
            in_specs=[
                pl.BlockSpec((tm, d), lambda i, j, l: (i, 0)),
                pl.BlockSpec((None, d, tn), lambda i, j, l: (l[0], 0, jnp.minimum(j, ja - 1))),
                pl.BlockSpec((None, d, tn), lambda i, j, l: (l[0], 0, jnp.maximum(j - ja, 0))),
                pl.BlockSpec((None, 1, LANES), lambda i, j, l: (l[0], 0, 0)),
                pl.BlockSpec((None, 1, LANES), lambda i, j, l: (l[0], 0, 0)),
            ],
            out_specs=pl.BlockSpec((tm, tn), lambda i, j, l: (i, j)),
        ),
        out_shape=jax.ShapeDtypeStruct((m, (ja + jb) * tn), F32),
        compiler_params=pltpu.CompilerParams(dimension_semantics=("arbitrary", "arbitrary")),
        name="proj",
    )(l_arr, h, w_a, w_b, qg, kg)


def _side_kernel(l_ref, h_ref, ws_ref, wgb_ref, bg_ref, bf_ref, la_ref, lf_ref, *rest,
                 n_heads, with_cumsum):
    s = _dot(h_ref[...], ws_ref[...])
    la_ref[...] = _log_sigmoid(_dot(s.astype(BF16), wgb_ref[...]) + bg_ref[...]) / GATE_TAU
    lf = _log_sigmoid(s + bf_ref[...])
    lf_ref[...] = lf
    if with_cumsum:
        fq_ref, ft_ref, carry_ref = rest
        ta = s.shape[0]

        @pl.when(pl.program_id(1) == 0)
        def _():
            carry_ref[...] = jnp.zeros_like(carry_ref)

        lower = (_iota((ta, ta), 0) >= _iota((ta, ta), 1)).astype(BF16)
        f = _dot_split(lf, lower, 3, left=True) + carry_ref[...]
        carry_ref[...] = f[ta - 1:ta, :]
        ft = f.T
        for hh in range(n_heads):
            c = SIDE_FC + hh
            fq_ref[:, hh * LANES:(hh + 1) * LANES] = jnp.broadcast_to(f[:, c:c + 1], (ta, LANES))
            ft_ref[hh] = ft[c:c + 1, :]


def _side(l_arr, h, w_side, wgb, bg, bfg, *, batch, ta, n_heads, with_cumsum):
    m, d = h.shape
    t = m // batch
    nt = t // ta
    n_la = wgb.shape[2]
    in_specs = [
        pl.BlockSpec((ta, d), lambda b, i, l: (b * nt + i, 0)),
        pl.BlockSpec((None, d, LANES), lambda b, i, l: (l[0], 0, 0)),
        pl.BlockSpec((None, LANES, n_la), lambda b, i, l: (l[0], 0, 0)),
        pl.BlockSpec((None, 1, n_la), lambda b, i, l: (l[0], 0, 0)),
        pl.BlockSpec((None, 1, LANES), lambda b, i, l: (l[0], 0, 0)),
    ]
    out_specs = [
        pl.BlockSpec((ta, n_la), lambda b, i, l: (b * nt + i, 0)),
        pl.BlockSpec((ta, LANES), lambda b, i, l: (b * nt + i, 0)),
    ]
    out_shape = [jax.ShapeDtypeStruct((m, n_la), F32), jax.ShapeDtypeStruct((m, LANES), F32)]
    scratch = []
    if with_cumsum:
        out_specs += [
            pl.BlockSpec((ta, n_heads * LANES), lambda b, i, l: (b * nt + i, 0)),
            pl.BlockSpec((None, n_heads, 1, ta), lambda b, i, l: (b, 0, 0, i)),
        ]
        out_shape += [jax.ShapeDtypeStruct((m, n_heads * LANES), F32),
                      jax.ShapeDtypeStruct((batch, n_heads, 1, t), F32)]
        scratch = [pltpu.VMEM((1, LANES), F32)]
    return pl.pallas_call(
        functools.partial(_side_kernel, n_heads=n_heads, with_cumsum=with_cumsum),
        grid_spec=pltpu.PrefetchScalarGridSpec(
            num_scalar_prefetch=1, grid=(batch, nt), in_specs=in_specs, out_specs=out_specs,
            scratch_shapes=scratch),
        out_shape=out_shape,
        compiler_params=pltpu.CompilerParams(dimension_semantics=("arbitrary", "arbitrary")),
        name="side",
    )(l_arr, h, w_side, wgb, bg, bfg)


def _interleave_rows(dst_ref, src_refs, t, chunk):
    n = len(src_refs)

    def body(c, carry):
        off = pl.multiple_of(c * chunk, chunk)
        for h, src in enumerate(src_refs):
            dst_ref[pl.ds(off * n + h, chunk, stride=n), :] = src[pl.ds(off, chunk), :]
        return carry

    lax.fori_loop(0, t // chunk, body, 0)


def _attn_specs(p, *, batch, n_heads, tb, cq, ck, cv, cg):
    m = p.shape[0]
    t = m // batch
    nq = t // tb
    w = n_heads * LANES
    in_specs = [
        pl.BlockSpec((tb, w), lambda b, i: (b * nq + i, cq // n_heads)),
        pl.BlockSpec((tb, w), lambda b, i: (b * nq + i, cg // n_heads)),
    ]
    in_specs += [pl.BlockSpec((t, LANES), lambda b, i, c=ck + h: (b, c)) for h in range(n_heads)]
    in_specs += [pl.BlockSpec((t, LANES), lambda b, i, c=cv + h: (b, c)) for h in range(n_heads)]
    out_specs = [
        pl.BlockSpec((tb, w), lambda b, i: (b * nq + i, 0)),
        pl.BlockSpec((None, t * n_heads, LANES), lambda b, i: (b, 0, 0)),
        pl.BlockSpec((None, t * n_heads, LANES), lambda b, i: (b, 0, 0)),
    ]
    out_shape = [
        jax.ShapeDtypeStruct((m, w), BF16),
        jax.ShapeDtypeStruct((batch, t * n_heads, LANES), F32),
        jax.ShapeDtypeStruct((batch, t * n_heads, LANES), F32),
    ]
    return (batch, nq), in_specs, out_specs, out_shape


def _sb_prompt_kernel(q_ref, g_ref, *refs, tb, n_heads, scale):
    k_refs = refs[:n_heads]
    v_refs = refs[n_heads:2 * n_heads]
    o_ref, ko_ref, vo_ref, qs_ref, acc_ref, carry_ref = refs[2 * n_heads:]
    i = pl.program_id(1)
    t = k_refs[0].shape[0]

    @pl.when(i == 0)
    def _():
        _interleave_rows(ko_ref, k_refs, t, tb)
        _interleave_rows(vo_ref, v_refs, t, tb)

    row = _iota((2 * LANES, 2 * LANES), 0) & (LANES - 1)
    col = _iota((2 * LANES, 2 * LANES), 1)
    later2 = jnp.logical_or(row > col, col >= LANES).astype(BF16)
    valid = _iota((tb, tb), 1) < _iota((tb, tb), 0)
    for h in range(n_heads):
        qs_ref[h] = (q_ref[:, h * LANES:(h + 1) * LANES] * scale).astype(BF16)
    acc_ref[...] = jnp.zeros_like(acc_ref)
    carry_ref[...] = jnp.zeros_like(carry_ref)

    def block(h, j, diagonal):
        off = pl.multiple_of(j * tb, tb)
        k = k_refs[h][pl.ds(off, tb), :].astype(BF16)
        v = v_refs[h][pl.ds(off, tb), :].astype(BF16)
        z = _dot_nt(qs_ref[h], k)
        l1m = -_softplus(z)
        if diagonal:
            l1m = jnp.where(valid, l1m, 0.0)
        run = carry_ref[h]
        tiles = [None] * (tb // LANES)
        for c in range(tb // LANES - 1, -1, -1):
            ls = slice(c * LANES, (c + 1) * LANES)
            cum = _dot(jnp.concatenate(_split_bf16(l1m[:, ls], 2), axis=1), later2)
            tiles[c] = jnp.exp(z[:, ls] + l1m[:, ls] + (cum[:, :LANES] + run))
            run = run + cum[:, LANES:]
        w = jnp.concatenate(tiles, axis=1)
        if diagonal:
            w = jnp.where(valid, w, 0.0)
        acc_ref[h] += _dot(w.astype(BF16), v)
        carry_ref[h] = run

    for h in range(n_heads):
        block(h, i, True)

    def body(jj, c):
        for h in range(n_heads):
            block(h, i - 1 - jj, False)
        return c

    lax.fori_loop(0, i, body, 0)
    for h in range(n_heads):
        cs = slice(h * LANES, (h + 1) * LANES)
        o_ref[:, cs] = (acc_ref[h] * _silu(g_ref[:, cs])).astype(o_ref.dtype)


def _sb_prompt(p, *, batch, n_heads, tb, cq, ck, cv, cg):
    grid, in_specs, out_specs, out_shape = _attn_specs(
        p, batch=batch, n_heads=n_heads, tb=tb, cq=cq, ck=ck, cv=cv, cg=cg)
    return pl.pallas_call(
        functools.partial(_sb_prompt_kernel, tb=tb, n_heads=n_heads, scale=LANES ** -0.5),
        grid=grid,
        in_specs=in_specs,
        out_specs=out_specs,
        out_shape=out_shape,
        scratch_shapes=[
            pltpu.VMEM((n_heads, tb, LANES), BF16),
            pltpu.VMEM((n_heads, tb, LANES), F32),
            pltpu.VMEM((n_heads, tb, LANES), F32),
        ],
        compiler_params=pltpu.CompilerParams(dimension_semantics=("arbitrary", "arbitrary")),
        name="sb_prompt",
    )(*([p] * (2 + 2 * n_heads)))


def _fox_prompt_kernel(q_ref, g_ref, *refs, tb, n_heads, scale):
    k_refs = refs[:n_heads]
    v_refs = refs[n_heads:2 * n_heads]
    fq_ref, ft_ref, o_ref, ko_ref, vo_ref, qs_ref, acc_ref, m_ref, l_ref = refs[2 * n_heads:]
    i = pl.program_id(1)
    t = k_refs[0].shape[0]

    @pl.when(i == 0)
    def _():
        _interleave_rows(ko_ref, k_refs, t, tb)
        _interleave_rows(vo_ref, v_refs, t, tb)

    valid = _iota((tb, tb), 1) <= _iota((tb, tb), 0)
    ones = jnp.ones((tb, LANES), BF16)
    wide = lambda a: jnp.concatenate([a] * (tb // LANES), axis=1)
    for h in range(n_heads):
        qs_ref[h] = (q_ref[:, h * LANES:(h + 1) * LANES] * scale).astype(BF16)
    acc_ref[...] = jnp.zeros_like(acc_ref)
    l_ref[...] = jnp.zeros_like(l_ref)
    m_ref[...] = jnp.full_like(m_ref, NEG_BIG)

    def block(h, j, diagonal):
        off = pl.multiple_of(j * tb, tb)
        k = k_refs[h][pl.ds(off, tb), :].astype(BF16)
        v1 = jnp.concatenate([v_refs[h][pl.ds(off, tb), :].astype(BF16), ones], axis=1)
        fq = fq_ref[:, h * LANES:(h + 1) * LANES]
        fk = ft_ref[h, :, pl.ds(off, tb)]
        z = _dot_nt(qs_ref[h], k) + (wide(fq) - fk)
        if diagonal:
            z = jnp.where(valid, z, NEG_BIG)
        m_old = m_ref[h]
        m_new = jnp.maximum(m_old, jnp.max(z, axis=1, keepdims=True))
        alpha = jnp.exp(m_old - m_new)
        pv = _dot(jnp.exp(z - wide(m_new)).astype(BF16), v1)
        acc_ref[h] = alpha * acc_ref[h] + pv[:, :LANES]
        l_ref[h] = alpha * l_ref[h] + pv[:, LANES:]
        m_ref[h] = m_new

    for h in range(n_heads):
        block(h, i, True)

    def body(jj, c):
        for h in range(n_heads):
            block(h, jj, False)
        return c

    lax.fori_loop(0, i, body, 0)
    for h in range(n_heads):
        cs = slice(h * LANES, (h + 1) * LANES)
        o_ref[:, cs] = ((acc_ref[h] / l_ref[h]) * _silu(g_ref[:, cs])).astype(o_ref.dtype)


def _fox_prompt(p, fq, ft, *, batch, n_heads, tb, cq, ck, cv, cg):
    grid, in_specs, out_specs, out_shape = _attn_specs(
        p, batch=batch, n_heads=n_heads, tb=tb, cq=cq, ck=ck, cv=cv, cg=cg)
    m = p.shape[0]
    t = m // batch
    nq = t // tb
    in_specs += [
        pl.BlockSpec((tb, n_heads * LANES), lambda b, i: (b * nq + i, 0)),
        pl.BlockSpec((None, n_heads, 1, t), lambda b, i: (b, 0, 0, 0)),
    ]
    return pl.pallas_call(
        functools.partial(_fox_prompt_kernel, tb=tb, n_heads=n_heads, scale=LANES ** -0.5),
        grid=grid,
        in_specs=in_specs,
        out_specs=out_specs,
        out_shape=out_shape,
        scratch_shapes=[
            pltpu.VMEM((n_heads, tb, LANES), BF16),
            pltpu.VMEM((n_heads, tb, LANES), F32),
            pltpu.VMEM((n_heads, tb, LANES), F32),
            pltpu.VMEM((n_heads, tb, LANES), F32),
        ],
        compiler_params=pltpu.CompilerParams(dimension_semantics=("arbitrary", "arbitrary")),
        name="fox_prompt",
    )(*([p] * (2 + 2 * n_heads)), fq, ft)


def _gla_prompt_kernel(l_ref, q_ref, k_ref, v_ref, g_ref, la_ref, ng_ref, y_ref, s_ref, st_ref, *,
                       tg, n_pairs, scale):
    c_len = GLA_CHUNK
    half = LANES // 2
    ti = pl.program_id(1)

    @pl.when(ti == 0)
    def _():
        st_ref[...] = jnp.zeros_like(st_ref)

    lower = (_iota((c_len, c_len), 0) >= _iota((c_len, c_len), 1)).astype(BF16)
    lane = _iota((c_len, LANES), 1)
    first = lane < half
    r2 = _iota((2 * c_len, 2 * c_len), 0)
    c2 = _iota((2 * c_len, 2 * c_len), 1)
    sh = c_len.bit_length() - 1
    att_mask = jnp.logical_and((r2 >> sh) == (c2 >> sh), (c2 & (c_len - 1)) <= (r2 & (c_len - 1)))
    lane_sq = _iota((LANES, LANES), 1) < half
    ng = ng_ref[...]

    for c in range(tg // c_len):
        rows = slice(c * c_len, (c + 1) * c_len)
        la = la_ref[rows, :]
        b = _dot_split(la, lower, 3, left=True)
        b_last = b[c_len - 1:c_len, :]
        qd = q_ref[rows, :] * scale * jnp.exp(b)
        kk = k_ref[rows, :]
        kd = kk * jnp.exp(-b)
        ku = kk * jnp.exp(b_last - b)
        e_last = jnp.exp(b_last)
        for pr in range(n_pairs):
            ls = slice(pr * LANES, (pr + 1) * LANES)
            qd_p = qd[:, ls]
            qs = jnp.concatenate([jnp.where(first, qd_p, 0.0), jnp.where(first, 0.0, qd_p)],
                                 axis=0).astype(BF16)
            kd_p = kd[:, ls].astype(BF16)
            kd2 = jnp.concatenate([kd_p, kd_p], axis=0)
            att = jnp.where(att_mask, _dot_nt(qs, kd2), 0.0).astype(BF16)
            v_pair = v_ref[rows, 2 * pr * LANES:(2 * pr + 2) * LANES].astype(BF16)
            v_stack = jnp.concatenate([v_pair[:, :LANES], v_pair[:, LANES:]], axis=0)
            st = st_ref[pr]
            o = _dot(att, v_stack) + _dot_nt(qs, st.astype(BF16))
            upd = _dot_tn(v_pair, ku[:, ls].astype(BF16))
            st_ref[pr] = st * e_last[:, ls] + jnp.where(lane_sq, upd[:LANES], upd[LANES:])
            for s in range(2):
                hh = 2 * pr + s
                cs = slice(hh * LANES, (hh + 1) * LANES)
                oh = _rms(o[s * c_len:(s + 1) * c_len], ng)
                y_ref[rows, cs] = (oh * _silu(g_ref[rows, cs])).astype(y_ref.dtype)

    @pl.when(ti == pl.num_programs(1) - 1)
    def _():
        for pr in range(n_pairs):
            s_t = st_ref[pr].T
            s_ref[2 * pr] = s_t[:half]
            s_ref[2 * pr + 1] = s_t[half:]


def _gla_prompt(l_arr, p, la, ng, *, batch, n_heads, dk, dv, tg, cq, ck, cv, cg):
    m = p.shape[0]
    t = m // batch
    nt = t // tg
    wk = n_heads * dk
    wv = n_heads * dv
    n_pairs = n_heads // 2
    return pl.pallas_call(
        functools.partial(_gla_prompt_kernel, tg=tg, n_pairs=n_pairs, scale=dk ** -0.5),
        grid_spec=pltpu.PrefetchScalarGridSpec(
            num_scalar_prefetch=1,
            grid=(batch, nt),
            in_specs=[
                pl.BlockSpec((tg, wk), lambda b, i, l: (b * nt + i, cq)),
                pl.BlockSpec((tg, wk), lambda b, i, l: (b * nt + i, ck)),
                pl.BlockSpec((tg, wv), lambda b, i, l: (b * nt + i, cv)),
                pl.BlockSpec((tg, wv), lambda b, i, l: (b * nt + i, cg)),
                pl.BlockSpec((tg, wk), lambda b, i, l: (b * nt + i, 0)),
                pl.BlockSpec((None, 1, dv), lambda b, i, l: (l[0], 0, 0)),
            ],
            out_specs=[
                pl.BlockSpec((tg, wv), lambda b, i, l: (b * nt + i, 0)),
                pl.BlockSpec((None, n_heads, dk, dv), lambda b, i, l: (b, 0, 0, 0)),
            ],
            scratch_shapes=[pltpu.VMEM((n_pairs, LANES, LANES), F32)],
        ),
        out_shape=[jax.ShapeDtypeStruct((m, wv), BF16),
                   jax.ShapeDtypeStruct((batch, n_heads, dk, dv), F32)],
        compiler_params=pltpu.CompilerParams(dimension_semantics=("arbitrary", "arbitrary")),
        name="gla_prompt",
    )(l_arr, p, p, p, p, la, ng)


def _out_kernel(l_ref, x_ref, ya_ref, yb_ref, yc_ref, w0, w1, w2, w3, gain_ref, o_ref, *h_ref,
                final):
    wq = w0.shape[0]
    yb = yb_ref[...].astype(BF16)
    acc = x_ref[...] + _dot(ya_ref[...].astype(BF16), w0[...])
    acc = acc + _dot(yb[:, :wq], w1[...]) + _dot(yb[:, wq:], w2[...])
    acc = acc + _dot(yc_ref[...].astype(BF16), w3[...])
    if final:
        o_ref[...] = _rms(acc, gain_ref[...])
    else:
        o_ref[...] = acc
        h_ref[0][...] = _rms(acc, gain_ref[...]).astype(BF16)


def _out_proj(l_arr, x, ya, yb, yc, w_out, gain, *, tm, final):
    m, d = x.shape
    wq = ya.shape[1]
    assert yb.shape[1] == 2 * wq and yc.shape[1] == wq and w_out.shape[1] == 4 * wq
    wspec = lambda r: pl.BlockSpec((None, wq, d), lambda i, l, r=r: (l[0], r, 0))
    row_spec = pl.BlockSpec((tm, d), lambda i, l: (i, 0))
    out_specs = row_spec if final else [row_spec, row_spec]
    out_shape = jax.ShapeDtypeStruct((m, d), F32)
    if not final:
        out_shape = [out_shape, jax.ShapeDtypeStruct((m, d), BF16)]
    return pl.pallas_call(
        functools.partial(_out_kernel, final=final),
        grid_spec=pltpu.PrefetchScalarGridSpec(
            num_scalar_prefetch=1,
            grid=(m // tm,),
            in_specs=[
                pl.BlockSpec((tm, d), lambda i, l: (i, 0)),
                pl.BlockSpec((tm, wq), lambda i, l: (i, 0)),
                pl.BlockSpec((tm, 2 * wq), lambda i, l: (i, 0)),
                pl.BlockSpec((tm, wq), lambda i, l: (i, 0)),
                wspec(0), wspec(1), wspec(2), wspec(3),
                pl.BlockSpec((1, d), lambda i, l: (0, 0)),
            ],
            out_specs=out_specs,
        ),
        out_shape=out_shape,
        compiler_params=pltpu.CompilerParams(dimension_semantics=("arbitrary",)),
        name="out_proj",
    )(l_arr, x, ya, yb, yc, w_out, w_out, w_out, w_out, gain)


def _rev_excl_cumsum(x, later, parts):
    r, n = x.shape
    nb = n // LANES
    xs = jnp.concatenate([x[:, i * LANES:(i + 1) * LANES] for i in range(nb)], axis=0)
    cs = _dot_split(xs, later, parts)
    tot = jnp.sum(xs, axis=1, keepdims=True)
    run = jnp.zeros((r, 1), F32)
    blocks = [None] * nb
    for i in range(nb - 1, -1, -1):
        blocks[i] = cs[i * r:(i + 1) * r] + run
        run = run + tot[i * r:(i + 1) * r]
    return jnp.concatenate(blocks, axis=1), run


def _head_rows(row_vec, h):
    part = jnp.broadcast_to(row_vec[:, h * LANES:(h + 1) * LANES], (SUBLANES, LANES))
    return jnp.where(_iota((SUBLANES, LANES), 0) == h, part, 0.0)


def _head_block(buf, slot, h, tok, n_heads):
    return buf[slot, pl.ds(h, tok, stride=n_heads), :].astype(BF16)


def _paged_scores(q_row, kbuf, slot, tok, n_heads):
    z = None
    for h in range(n_heads):
        t = _dot_nt(_head_rows(q_row, h).astype(BF16), _head_block(kbuf, slot, h, tok, n_heads))
        z = t if z is None else z + t
    return z


def _own_rows(acc_ref, n_heads, denom=None):
    parts = []
    for h in range(n_heads):
        a = acc_ref[h] if denom is None else acc_ref[h] / denom
        parts.append(a[h:h + 1, :])
    return jnp.concatenate(parts, axis=1)


def _page_copies(pt_ref, l, bb, gg, slot, srcs, bufs, sem, *, n_groups, g_pages):
    base = (n_groups - 1 - gg) * g_pages
    out = []
    for r in range(g_pages):
        pg = pt_ref[bb, base + r]
        for si, (src, buf) in enumerate(zip(srcs, bufs)):
            rows = src.shape[2]
            out.append(pltpu.make_async_copy(
                src.at[l, pg], buf.at[slot, pl.ds(r * rows, rows)], sem.at[si, slot]))
    return out


def _paged_pipeline(pt_ref, l, srcs, bufs, sem, *, n_groups, g_pages):
    bb = pl.program_id(0)
    gg = pl.program_id(1)
    step = bb * n_groups + gg
    total = pl.num_programs(0) * n_groups
    slot = lax.rem(step, 2)
    mk = functools.partial(_page_copies, pt_ref, l, srcs=srcs, bufs=bufs, sem=sem,
                           n_groups=n_groups, g_pages=g_pages)

    @pl.when(step == 0)
    def _():
        for cp in mk(bb, gg, slot):
            cp.start()

    @pl.when(step + 1 < total)
    def _():
        wrap = gg + 1 == n_groups
        nb = jnp.where(wrap, bb + 1, bb)
        ng = jnp.where(wrap, 0, gg + 1)
        for cp in mk(nb, ng, 1 - slot):
            cp.start()

    for cp in mk(bb, gg, slot):
        cp.wait()
    return slot


def _sb_decode_kernel(pt_ref, l_ref, p_ref, kc_ref, vc_ref, o_ref, kbuf, vbuf, sem, acc_ref,
                      carry_ref, *, n_groups, g_pages, tok, n_heads, scale, cq, cg):
    gg = pl.program_id(1)
    w = n_heads * LANES
    slot = _paged_pipeline(pt_ref, l_ref[0], (kc_ref, vc_ref), (kbuf, vbuf), sem,
                           n_groups=n_groups, g_pages=g_pages)

    @pl.when(gg == 0)
    def _():
        acc_ref[...] = jnp.zeros_like(acc_ref)
        carry_ref[...] = jnp.zeros_like(carry_ref)

    q_row = p_ref[:, cq * LANES:cq * LANES + w] * scale
    later = (_iota((LANES, LANES), 0) > _iota((LANES, LANES), 1)).astype(BF16)
    z = _paged_scores(q_row, kbuf, slot, tok, n_heads)
    l1m = -_softplus(z)
    cum, tot = _rev_excl_cumsum(l1m, later, 2)
    wgt = jnp.exp(z + l1m + cum + carry_ref[...]).astype(BF16)
    for h in range(n_heads):
        acc_ref[h] += _dot(wgt, _head_block(vbuf, slot, h, tok, n_heads))
    carry_ref[...] += tot

    @pl.when(gg == n_groups - 1)
    def _():
        gate = p_ref[:, cg * LANES:cg * LANES + w]
        o_ref[...] = _own_rows(acc_ref, n_heads) * _silu(gate)


def _sb_decode(page_table, l_arr, p, kc, vc, *, n_heads, g_pages, cq, cg):
    n_seq, n_pages = page_table.shape
    rows = kc.shape[2]
    w = n_heads * LANES
    n_groups = n_pages // g_pages
    tok = g_pages * rows // n_heads
    return pl.pallas_call(
        functools.partial(_sb_decode_kernel, n_groups=n_groups, g_pages=g_pages, tok=tok,
                          n_heads=n_heads, scale=LANES ** -0.5, cq=cq, cg=cg),
        grid_spec=pltpu.PrefetchScalarGridSpec(
            num_scalar_prefetch=2,
            grid=(n_seq, n_groups),
            in_specs=[
                pl.BlockSpec((None, 1, p.shape[2]), lambda b, g, pt, l: (b, 0, 0)),
                pl.BlockSpec(memory_space=pl.ANY),
                pl.BlockSpec(memory_space=pl.ANY),
            ],
            out_specs=pl.BlockSpec((None, 1, w), lambda b, g, pt, l: (b, 0, 0)),
            scratch_shapes=[
                pltpu.VMEM((2, g_pages * rows, LANES), F32),
                pltpu.VMEM((2, g_pages * rows, LANES), F32),
                pltpu.SemaphoreType.DMA((2, 2)),
                pltpu.VMEM((n_heads, SUBLANES, LANES), F32),
                pltpu.VMEM((SUBLANES, 1), F32),
            ],
        ),
        out_shape=jax.ShapeDtypeStruct((n_seq, 1, w), F32),
        compiler_params=pltpu.CompilerParams(dimension_semantics=("arbitrary", "arbitrary")),
        name="sb_decode",
    )(page_table, l_arr, p, kc, vc)


def _fox_decode_kernel(pt_ref, l_ref, p_ref, lfn_ref, kc_ref, vc_ref, lc_ref, o_ref, kbuf, vbuf,
                       lbuf, sem, acc_ref, m_ref, l_sum_ref, carry_ref, *, n_groups, g_pages, tok,
                       n_heads, scale, cq, ck, cv, cg):
    gg = pl.program_id(1)
    w = n_heads * LANES
    slot = _paged_pipeline(pt_ref, l_ref[0], (kc_ref, vc_ref, lc_ref), (kbuf, vbuf, lbuf), sem,
                           n_groups=n_groups, g_pages=g_pages)
    q_row = p_ref[:, cq * LANES:cq * LANES + w] * scale

    @pl.when(gg == 0)
    def _():
        k_new = p_ref[:, ck * LANES:ck * LANES + w]
        v_new = p_ref[:, cv * LANES:cv * LANES + w]
        z_new = jnp.zeros((SUBLANES, 1), F32)
        for h in range(n_heads):
            cs = slice(h * LANES, (h + 1) * LANES)
            z_new = z_new + jnp.sum(_head_rows(q_row, h) * k_new[:, cs], axis=1, keepdims=True)
            acc_ref[h] = jnp.broadcast_to(v_new[:, cs], (SUBLANES, LANES))
        m_ref[...] = z_new
        l_sum_ref[...] = jnp.ones_like(l_sum_ref)
        pick = _iota((SUBLANES, LANES), 1) == _iota((SUBLANES, LANES), 0) + SIDE_FC
        lf_row = jnp.broadcast_to(lfn_ref[...], (SUBLANES, LANES))
        carry_ref[...] = jnp.sum(jnp.where(pick, lf_row, 0.0), axis=1, keepdims=True)

    later = (_iota((LANES, LANES), 0) > _iota((LANES, LANES), 1)).astype(BF16)
    lf = jnp.concatenate([lbuf[slot, pl.ds(r * SUBLANES, SUBLANES), :] for r in range(g_pages)],
                         axis=1)
    bias, tot = _rev_excl_cumsum(lf, later, 3)
    z = _paged_scores(q_row, kbuf, slot, tok, n_heads) + (bias + carry_ref[...])
    m_old = m_ref[...]
    m_new = jnp.maximum(m_old, jnp.max(z, axis=1, keepdims=True))
    alpha = jnp.exp(m_old - m_new)
    pr = jnp.exp(z - m_new)
    l_sum_ref[...] = alpha * l_sum_ref[...] + jnp.sum(pr, axis=1, keepdims=True)
    pr = pr.astype(BF16)
    for h in range(n_heads):
        acc_ref[h] = alpha * acc_ref[h] + _dot(pr, _head_block(vbuf, slot, h, tok, n_heads))
    m_ref[...] = m_new
    carry_ref[...] += tot

    @pl.when(gg == n_groups - 1)
    def _():
        gate = p_ref[:, cg * LANES:cg * LANES + w]
        o_ref[...] = _own_rows(acc_ref, n_heads, l_sum_ref[...]) * _silu(gate)


def _fox_decode(page_table, l_arr, p, lf_new, kc, vc, lc, *, n_heads, g_pages, cq, ck, cv, cg):
    n_seq, n_pages = page_table.shape
    rows = kc.shape[2]
    w = n_heads * LANES
    n_groups = n_pages // g_pages
    tok = g_pages * rows // n_heads
    return pl.pallas_call(
        functools.partial(_fox_decode_kernel, n_groups=n_groups, g_pages=g_pages, tok=tok,
                          n_heads=n_heads, scale=LANES ** -0.5, cq=cq, ck=ck, cv=cv, cg=cg),
        grid_spec=pltpu.PrefetchScalarGridSpec(
            num_scalar_prefetch=2,
            grid=(n_seq, n_groups),
            in_specs=[
                pl.BlockSpec((None, 1, p.shape[2]), lambda b, g, pt, l: (b, 0, 0)),
                pl.BlockSpec((None, 1, lf_new.shape[2]), lambda b, g, pt, l: (b, 0, 0)),
                pl.BlockSpec(memory_space=pl.ANY),
                pl.BlockSpec(memory_space=pl.ANY),
                pl.BlockSpec(memory_space=pl.ANY),
            ],
            out_specs=pl.BlockSpec((None, 1, w), lambda b, g, pt, l: (b, 0, 0)),
            scratch_shapes=[
                pltpu.VMEM((2, g_pages * rows, LANES), F32),
                pltpu.VMEM((2, g_pages * rows, LANES), F32),
                pltpu.VMEM((2, g_pages * SUBLANES, LANES), F32),
                pltpu.SemaphoreType.DMA((3, 2)),
                pltpu.VMEM((n_heads, SUBLANES, LANES), F32),
                pltpu.VMEM((SUBLANES, 1), F32),
                pltpu.VMEM((SUBLANES, 1), F32),
                pltpu.VMEM((SUBLANES, 1), F32),
            ],
        ),
        out_shape=jax.ShapeDtypeStruct((n_seq, 1, w), F32),
        compiler_params=pltpu.CompilerParams(dimension_semantics=("arbitrary", "arbitrary")),
        name="fox_decode",
    )(page_table, l_arr, p, lf_new, kc, vc, lc)


def _gla_decode_kernel(l_ref, p_ref, la_ref, s0_ref, ng_ref, y_ref, s_ref, *, n_heads, dk, dv,
                       scale, cq, ck, cv, cg):
    bb = pl.program_id(0)
    wk = n_heads * dk

    q_row = p_ref[:,cq * LANES:cq * LANES + wk]
    k_row = p_ref[:,ck * LANES:ck * LANES + wk]
    e_row = jnp.exp(la_ref[...])
    lane = _iota((dk, wk), 1)
    sub = _iota((dk, wk), 0)
    ng = ng_ref[...]

    def column(row_vec, hh):
        pick = lane == sub + hh * dk
        return jnp.sum(jnp.where(pick, jnp.broadcast_to(row_vec, (dk, wk)), 0.0),
                       axis=1, keepdims=True)

    for hh in range(n_heads):
        cs = slice(hh * dv, (hh + 1) * dv)
        v_row = p_ref[:,cv * LANES + hh * dv:cv * LANES + (hh + 1) * dv]
        s_new = column(e_row, hh) * s0_ref[hh] + column(k_row, hh) * v_row
        s_ref[hh] = s_new
        o = jnp.sum((column(q_row, hh) * scale) * s_new, axis=0, keepdims=True)
        gate = p_ref[:,cg * LANES + hh * dv:cg * LANES + (hh + 1) * dv]
        y_ref[:, cs] = _rms(o, ng) * _silu(gate)


def _gla_decode(l_arr, p, la, state, ng, *, cq, ck, cv, cg):
    _, n_seq, n_heads, dk, dv = state.shape
    return pl.pallas_call(
        functools.partial(_gla_decode_kernel, n_heads=n_heads, dk=dk, dv=dv, scale=dk ** -0.5,
                          cq=cq, ck=ck, cv=cv, cg=cg),
        grid_spec=pltpu.PrefetchScalarGridSpec(
            num_scalar_prefetch=1,
            grid=(n_seq,),
            in_specs=[
                pl.BlockSpec((None, 1, p.shape[2]), lambda b, l: (b, 0, 0)),
                pl.BlockSpec((None, 1, la.shape[2]), lambda b, l: (b, 0, 0)),
                pl.BlockSpec((None, None, n_heads, dk, dv), lambda b, l: (l[0], b, 0, 0, 0)),
                pl.BlockSpec((None, 1, dv), lambda b, l: (l[0], 0, 0)),
            ],
            out_specs=[
                pl.BlockSpec((None, 1, n_heads * dv), lambda b, l: (b, 0, 0)),
                pl.BlockSpec((None, n_heads, dk, dv), lambda b, l: (b, 0, 0, 0)),
            ],
        ),
        out_shape=[jax.ShapeDtypeStruct((n_seq, 1, n_heads * dv), F32),
                   jax.ShapeDtypeStruct((n_seq, n_heads, dk, dv), F32)],
        compiler_params=pltpu.CompilerParams(dimension_semantics=("arbitrary",)),
        name="gla_decode",
    )(l_arr, p, la, state, ng)


DEC_ROWS = 16
PROJ_TM = 1024
SIDE_TA = 256
ATT_TB = 256
GLA_TG = 256
OUT_TM = 512
PAGES_PER_STEP = 16


def kernel(x_prompt, x_sample, cache_sb_k, cache_sb_v, cache_fox_k, cache_fox_v, cache_fox_logf,
           state_gla, page_table, norm_g, w_in, w_gate_b, b_gate, b_forget, q_norm_g, k_norm_g,
           gla_norm_g, w_out, norm_f):
    batch, seq, d = x_prompt.shape
    n_seq = x_sample.shape[0]
    depth, n_pool, page, h_sb, hd = cache_sb_k.shape
    h_fox = cache_fox_k.shape[3]
    _, _, h_gla, dk, dv = state_gla.shape
    rank = w_gate_b.shape[1]
    assert hd == LANES and dv == LANES and 2 * dk == LANES and x_sample.shape[1] == 1
    w_sb, w_fox, w_gk, w_gv = h_sb * hd, h_fox * hd, h_gla * dk, h_gla * dv
    assert w_sb == w_fox == w_gk and w_gv == 2 * w_sb

    sizes = (w_sb,) * 4 + (w_gk, w_gk, w_gv, w_gv, rank) + (w_fox,) * 4 + (h_fox,)
    offs = [0]
    for s in sizes:
        offs.append(offs[-1] + s)
    assert offs[-1] == w_in.shape[2]
    o_ab, o_qc, o_fc = offs[8], offs[9], offs[13]
    n_main = o_ab + (o_fc - o_qc)
    c_qa, c_ka, c_va, c_ga = (offs[i] // LANES for i in range(4))
    c_qb, c_kb, c_vb, c_gb = (offs[i] // LANES for i in range(4, 8))
    c_qc, c_kc, c_vc, c_gc = ((offs[i] - rank) // LANES for i in range(9, 13))

    w_a = w_in[:, :, :o_ab].astype(BF16)
    w_b = w_in[:, :, o_qc:o_fc].astype(BF16)
    w_side = jnp.zeros((depth, d, LANES), F32)
    w_side = w_side.at[:, :, SIDE_FC:SIDE_FC + h_fox].set(w_in[:, :, o_fc:])
    w_side = w_side.at[:, :, SIDE_AB:SIDE_AB + rank].set(w_in[:, :, o_ab:o_qc]).astype(BF16)
    wgb = jnp.zeros((depth, LANES, w_gk), F32).at[:, SIDE_AB:SIDE_AB + rank].set(w_gate_b)
    wgb = wgb.astype(BF16)
    bg = b_gate.reshape(depth, 1, w_gk)
    bfg = jnp.zeros((depth, 1, LANES), F32).at[:, 0, SIDE_FC:SIDE_FC + h_fox].set(b_forget)
    w_o = w_out.astype(BF16)
    ng3 = norm_g.reshape(depth, 1, d)
    qg3 = q_norm_g.reshape(depth, 1, hd)
    kg3 = k_norm_g.reshape(depth, 1, hd)
    gg3 = gla_norm_g.reshape(depth, 1, dv)
    nf2 = norm_f.reshape(1, d)

    kc_sb = cache_sb_k.reshape(depth, n_pool, page * h_sb, hd)
    vc_sb = cache_sb_v.reshape(depth, n_pool, page * h_sb, hd)
    kc_fx = cache_fox_k.reshape(depth, n_pool, page * h_fox, hd)
    vc_fx = cache_fox_v.reshape(depth, n_pool, page * h_fox, hd)
    lc_fx = jnp.pad(jnp.swapaxes(cache_fox_logf, 2, 3), ((0, 0), (0, 0), (0, SUBLANES - h_fox), (0, 0)))

    m_p = batch * seq
    hp = x_prompt.reshape(m_p, d)
    hs = jnp.pad(x_sample.reshape(n_seq, d), ((0, DEC_ROWS - n_seq), (0, 0)))

    tn = 2 * w_fox
    assert o_ab % tn == 0 and (o_fc - o_qc) % tn == 0
    proj = functools.partial(_proj, tn=tn)

    l0 = jnp.zeros((1,), jnp.int32)
    hp_n = _norm(l0, hp, ng3, tm=min(OUT_TM, m_p))
    hs_n = _norm(l0, hs, ng3, tm=DEC_ROWS)
    outs = [[] for _ in range(12)]
    for layer in range(depth):
        l_arr = jnp.full((1,), layer, jnp.int32)
        final = layer == depth - 1
        gain = nf2 if final else ng3[layer + 1]

        p = proj(l_arr, hp_n, w_a, w_b, qg3, kg3, tm=min(PROJ_TM, m_p))
        la, lf, fq, ft = _side(l_arr, hp_n, w_side, wgb, bg, bfg, batch=batch,
                               ta=min(SIDE_TA, seq), n_heads=h_fox, with_cumsum=True)
        tb = min(ATT_TB, seq)
        ya, ka_o, va_o = _sb_prompt(p, batch=batch, n_heads=h_sb, tb=tb,
                                    cq=c_qa, ck=c_ka, cv=c_va, cg=c_ga)
        yb, s_fin = _gla_prompt(l_arr, p, la, gg3, batch=batch, n_heads=h_gla, dk=dk, dv=dv,
                                tg=min(GLA_TG, seq), cq=c_qb * LANES // w_gk, ck=c_kb * LANES // w_gk,
                                cv=c_vb * LANES // w_gv, cg=c_gb * LANES // w_gv)
        yc, kc_o, vc_o = _fox_prompt(p, fq, ft, batch=batch, n_heads=h_fox, tb=tb,
                         cq=c_qc, ck=c_kc, cv=c_vc, cg=c_gc)
        hp = _out_proj(l_arr, hp, ya, yb, yc, w_o, gain, tm=min(OUT_TM, m_p), final=final)
        if not final:
            hp, hp_n = hp
        outs[0].append(ka_o.reshape(batch, seq, h_sb, hd))
        outs[1].append(va_o.reshape(batch, seq, h_sb, hd))
        outs[2].append(kc_o.reshape(batch, seq, h_fox, hd))
        outs[3].append(vc_o.reshape(batch, seq, h_fox, hd))
        outs[4].append(lf[:, SIDE_FC:SIDE_FC + h_fox].reshape(batch, seq, h_fox))
        outs[5].append(s_fin)

        ps = proj(l_arr, hs_n, w_a, w_b, qg3, kg3, tm=DEC_ROWS)
        las, lfs = _side(l_arr, hs_n, w_side, wgb, bg, bfg, batch=1, ta=DEC_ROWS,
                         n_heads=h_fox, with_cumsum=False)
        ps3 = ps.reshape(DEC_ROWS, 1, -1)
        g_pages = min(PAGES_PER_STEP, page_table.shape[1])
        yas = _sb_decode(page_table, l_arr, ps3, kc_sb, vc_sb, n_heads=h_sb, g_pages=g_pages,
                         cq=c_qa, cg=c_ga)
        ybs, s_new = _gla_decode(l_arr, ps3, las.reshape(DEC_ROWS, 1, -1), state_gla, gg3,
                                 cq=c_qb, ck=c_kb, cv=c_vb, cg=c_gb)
        ycs = _fox_decode(page_table, l_arr, ps3, lfs.reshape(DEC_ROWS, 1, -1), kc_fx, vc_fx, lc_fx,
                          n_heads=h_fox, g_pages=g_pages, cq=c_qc, ck=c_kc, cv=c_vc, cg=c_gc)
        pad_rows = lambda y: jnp.pad(y.reshape(n_seq, -1), ((0, DEC_ROWS - n_seq), (0, 0)))
        hs = _out_proj(l_arr, hs, pad_rows(yas), pad_rows(ybs), pad_rows(ycs), w_o, gain,
                       tm=DEC_ROWS, final=final)
        if not final:
            hs, hs_n = hs
        cols = lambda c, w: ps[:n_seq, c * LANES:c * LANES + w]
        outs[6].append(cols(c_ka, w_sb).reshape(n_seq, 1, h_sb, hd))
        outs[7].append(cols(c_va, w_sb).reshape(n_seq, 1, h_sb, hd))
        outs[8].append(cols(c_kc, w_fox).reshape(n_seq, 1, h_fox, hd))
        outs[9].append(cols(c_vc, w_fox).reshape(n_seq, 1, h_fox, hd))
        outs[10].append(lfs[:n_seq, SIDE_FC:SIDE_FC + h_fox].reshape(n_seq, 1, h_fox))
        outs[11].append(s_new)

    y_prompt = hp.reshape(batch, seq, d)
    y_sample = hs[:n_seq].reshape(n_seq, 1, d)
    return (y_prompt, y_sample) + tuple(jnp.stack(o) for o in outs)
```

```python
import functools

import jax
import jax.numpy as jnp
from jax import lax
from jax.experimental import pallas as pl
from jax.experimental.pallas import tpu as pltpu

F32 = jnp.float32
BF16 = jnp.bfloat16

LANES = 128
SUBLANES = 8
RMS_EPS = 1e-6
GATE_TAU = 16.0
GLA_CHUNK = 64
NEG_BIG = -1e30
SB_DEAD = -110.0

SIDE_FC = 0
SIDE_AB = 8


def _dot(a, b):
    return jnp.dot(a, b, preferred_element_type=F32)


def _dot_nt(a, b):
    return lax.dot_general(a, b, (((1,), (1,)), ((), ())), preferred_element_type=F32)


def _dot_tn(a, b):
    return lax.dot_general(a, b, (((0,), (0,)), ((), ())), preferred_element_type=F32)


def _softplus(z):
    return jnp.maximum(z, 0.0) + jnp.log(1.0 + jnp.exp(-jnp.abs(z)))


def _log_sigmoid(z):
    return -_softplus(-z)


def _silu(g):
    return g / (1.0 + jnp.exp(-g))


def _rms(x, g):
    ms = jnp.mean(x * x, axis=-1, keepdims=True)
    return x * lax.rsqrt(ms + RMS_EPS) * g


def _split_bf16(x, parts):
    out = []
    r = x
    for _ in range(parts):
        p = r.astype(BF16)
        out.append(p)
        r = r - p.astype(F32)
    return out


def _dot_split(x, m, parts, left=False):
    acc = None
    for p in _split_bf16(x, parts):
        t = _dot(m, p) if left else _dot(p, m)
        acc = t if acc is None else acc + t
    return acc


def _iota(shape, dim):
    return lax.broadcasted_iota(jnp.int32, shape, dim)


def _norm_kernel(l_ref, x_ref, g_ref, h_ref):
    h_ref[...] = _rms(x_ref[...], g_ref[...]).astype(BF16)


def _norm(l_arr, x, norm_g, *, tm):
    m, d = x.shape
    return pl.pallas_call(
        _norm_kernel,
        grid_spec=pltpu.PrefetchScalarGridSpec(
            num_scalar_prefetch=1,
            grid=(m // tm,),
            in_specs=[
                pl.BlockSpec((tm, d), lambda i, l: (i, 0)),
                pl.BlockSpec((None, 1, d), lambda i, l: (l[0], 0, 0)),
            ],
            out_specs=pl.BlockSpec((tm, d), lambda i, l: (i, 0)),
        ),
        out_shape=jax.ShapeDtypeStruct((m, d), BF16),
        compiler_params=pltpu.CompilerParams(dimension_semantics=("arbitrary",)),
        name="norm",
    )(l_arr, x, norm_g)


def _proj_kernel(l_ref, h_ref, wa_ref, wb_ref, qg_ref, kg_ref, o_ref, w_ref, *, ja, jn):
    j = pl.program_id(0)
    i = pl.program_id(1)

    @pl.when(jnp.logical_and(i == 0, j < ja))
    def _():
        w_ref[...] = wa_ref[...].astype(BF16)

    @pl.when(jnp.logical_and(i == 0, j >= ja))
    def _():
        w_ref[...] = wb_ref[...]

    acc = _dot(h_ref[...], w_ref[...])

    @pl.when(j != jn)
    def _():
        o_ref[...] = acc

    @pl.when(j == jn)
    def _():
        n_h = acc.shape[1] // LANES
        for hh in range(n_h):
            cs = slice(hh * LANES, (hh + 1) * LANES)
            g = qg_ref[...] if hh < n_h // 2 else kg_ref[...]
            o_ref[:, cs] = _rms(acc[:, cs], g)


def _proj(l_arr, h, w_in, w_b, qg, kg, *, tm, tn, ja):
    m, d = h.shape
    jb = w_b.shape[2] // tn
    jn = ja
    return pl.pallas_call(
        functools.partial(_proj_kernel, ja=ja, jn=jn),
        grid_spec=pltpu.PrefetchScalarGridSpec(
            num_scalar_prefetch=1,
            grid=(ja + jb, m // tm),
            in_specs=[
                pl.BlockSpec((tm, d), lambda j, i, l: (i, 0)),
                pl.BlockSpec((None, d, tn), lambda j, i, l: (l[0], 0, jnp.minimum(j, ja - 1))),
                pl.BlockSpec((None, d, tn), lambda j, i, l: (l[0], 0, jnp.maximum(j - ja, 0))),
                pl.BlockSpec((None, 1, LANES), lambda j, i, l: (l[0], 0, 0)),
                pl.BlockSpec((None, 1, LANES), lambda j, i, l: (l[0], 0, 0)),
            ],
            out_specs=pl.BlockSpec((tm, tn), lambda j, i, l: (i, j)),
            scratch_shapes=[pltpu.VMEM((d, tn), BF16)],
        ),
        out_shape=jax.ShapeDtypeStruct((m, (ja + jb) * tn), F32),
        compiler_params=pltpu.CompilerParams(dimension_semantics=("arbitrary", "arbitrary")),
        name="proj",
    )(l_arr, h, w_in, w_b, qg, kg)


def _side_kernel(l_ref, h_ref, ws_ref, wgb_ref, bg_ref, bf_ref, la_ref, lf_ref, *rest,
                 n_heads, with_cumsum):
    s = _dot(h_ref[...], ws_ref[...])
    la_ref[...] = _log_sigmoid(_dot(s.astype(BF16), wgb_ref[...]) + bg_ref[...]) / GATE_TAU
    lf = _log_sigmoid(s + bf_ref[...])
    lf_ref[...] = lf
    if with_cumsum:
        fq_ref, ft_ref, carry_ref = rest
        ta = s.shape[0]

        @pl.when(pl.program_id(1) == 0)
        def _():
            carry_ref[...] = jnp.zeros_like(carry_ref)

        lower = (_iota((ta, ta), 0) >= _iota((ta, ta), 1)).astype(BF16)
        f = _dot_split(lf, lower, 3, left=True) + carry_ref[...]
        carry_ref[...] = f[ta - 1:ta, :]
        ft = f.T
        for hh in range(n_heads):
            c = SIDE_FC + hh
            fq_ref[:, hh * LANES:(hh + 1) * LANES] = jnp.broadcast_to(f[:, c:c + 1], (ta, LANES))
            ft_ref[hh] = ft[c:c + 1, :]


def _side(l_arr, h, w_side, wgb, bg, bfg, *, batch, ta, n_heads, with_cumsum):
    m, d = h.shape
    t = m // batch
    nt = t // ta
    n_la = wgb.shape[2]
    in_specs = [
        pl.BlockSpec((ta, d), lambda b, i, l: (b * nt + i, 0)),
        pl.BlockSpec((None, d, LANES), lambda b, i, l: (l[0], 0, 0)),
        pl.BlockSpec((None, LANES, n_la), lambda b, i, l: (l[0], 0, 0)),
        pl.BlockSpec((None, 1, n_la), lambda b, i, l: (l[0], 0, 0)),
        pl.BlockSpec((None, 1, LANES), lambda b, i, l: (l[0], 0, 0)),
    ]
    out_specs = [
        pl.BlockSpec((ta, n_la), lambda b, i, l: (b * nt + i, 0)),
        pl.BlockSpec((ta, LANES), lambda b, i, l: (b * nt + i, 0)),
    ]
    out_shape = [jax.ShapeDtypeStruct((m, n_la), F32), jax.ShapeDtypeStruct((m, LANES), F32)]
    scratch = []
    if with_cumsum:
        out_specs += [
            pl.BlockSpec((ta, n_heads * LANES), lambda b, i, l: (b * nt + i, 0)),
            pl.BlockSpec((None, n_heads, 1, ta), lambda b, i, l: (b, 0, 0, i)),
        ]
        out_shape += [jax.ShapeDtypeStruct((m, n_heads * LANES), F32),
                      jax.ShapeDtypeStruct((batch, n_heads, 1, t), F32)]
        scratch = [pltpu.VMEM((1, LANES), F32)]
    return pl.pallas_call(
        functools.partial(_side_kernel, n_heads=n_heads, with_cumsum=with_cumsum),
        grid_spec=pltpu.PrefetchScalarGridSpec(
            num_scalar_prefetch=1, grid=(batch, nt), in_specs=in_specs, out_specs=out_specs,
            scratch_shapes=scratch),
        out_shape=out_shape,
        compiler_params=pltpu.CompilerParams(dimension_semantics=("arbitrary", "arbitrary")),
        name="side",
    )(l_arr, h, w_side, wgb, bg, bfg)


def _interleave_rows(dst_ref, src_refs, t, chunk):
    n = len(src_refs)

    def body(c, carry):
        off = pl.multiple_of(c * chunk, chunk)
        for h, src in enumerate(src_refs):
            dst_ref[pl.ds(off * n + h, chunk, stride=n), :] = src[pl.ds(off, chunk), :]
        return carry

    lax.fori_loop(0, t // chunk, body, 0)


def _attn_specs(p, *, batch, n_heads, tb, cq, ck, cv, cg):
    m = p.shape[0]
    t = m // batch
    nq = t // tb
    w = n_heads * LANES
    in_specs = [
        pl.BlockSpec((tb, w), lambda b, i: (b * nq + i, cq // n_heads)),
        pl.BlockSpec((tb, w), lambda b, i: (b * nq + i, cg // n_heads)),
    ]
    in_specs += [pl.BlockSpec((t, LANES), lambda b, i, c=ck + h: (b, c)) for h in range(n_heads)]
    in_specs += [pl.BlockSpec((t, LANES), lambda b, i, c=cv + h: (b, c)) for h in range(n_heads)]
    out_specs = [
        pl.BlockSpec((tb, w), lambda b, i: (b * nq + i, 0)),
        pl.BlockSpec((None, t * n_heads, LANES), lambda b, i: (b, 0, 0)),
        pl.BlockSpec((None, t * n_heads, LANES), lambda b, i: (b, 0, 0)),
    ]
    out_shape = [
        jax.ShapeDtypeStruct((m, w), BF16),
        jax.ShapeDtypeStruct((batch, t * n_heads, LANES), F32),
        jax.ShapeDtypeStruct((batch, t * n_heads, LANES), F32),
    ]
    return (batch, nq), in_specs, out_specs, out_shape


def _sb_prompt_kernel(q_ref, g_ref, *refs, tb, n_heads, scale):
    k_refs = refs[:n_heads]
    v_refs = refs[n_heads:2 * n_heads]
    o_ref, ko_ref, vo_ref, qs_ref, acc_ref, carry_ref = refs[2 * n_heads:]
    i = pl.program_id(1)
    t = k_refs[0].shape[0]

    @pl.when(i == 0)
    def _():
        _interleave_rows(ko_ref, k_refs, t, tb)
        _interleave_rows(vo_ref, v_refs, t, tb)

    row = _iota((2 * LANES, 2 * LANES), 0) & (LANES - 1)
    col = _iota((2 * LANES, 2 * LANES), 1)
    later2 = jnp.logical_or(row > col, col >= LANES).astype(BF16)
    valid = _iota((tb, tb), 1) < _iota((tb, tb), 0)
    for h in range(n_heads):
        qs_ref[h] = (q_ref[:, h * LANES:(h + 1) * LANES] * scale).astype(BF16)
    acc_ref[...] = jnp.zeros_like(acc_ref)
    carry_ref[...] = jnp.zeros_like(carry_ref)

    def block(h, j, diagonal):
        off = pl.multiple_of(j * tb, tb)
        k = k_refs[h][pl.ds(off, tb), :].astype(BF16)
        v = v_refs[h][pl.ds(off, tb), :].astype(BF16)
        z = _dot_nt(qs_ref[h], k)
        l1m = -_softplus(z)
        if diagonal:
            l1m = jnp.where(valid, l1m, 0.0)
        run = carry_ref[h]
        tiles = [None] * (tb // LANES)
        for c in range(tb // LANES - 1, -1, -1):
            ls = slice(c * LANES, (c + 1) * LANES)
            cum = _dot(jnp.concatenate(_split_bf16(l1m[:, ls], 2), axis=1), later2)
            tiles[c] = jnp.exp(z[:, ls] + l1m[:, ls] + (cum[:, :LANES] + run))
            run = run + cum[:, LANES:]
        w = jnp.concatenate(tiles, axis=1)
        if diagonal:
            w = jnp.where(valid, w, 0.0)
        acc_ref[h] += _dot(w.astype(BF16), v)
        carry_ref[h] = run

    def any_live():
        top = carry_ref[0]
        for h in range(1, n_heads):
            top = jnp.maximum(top, carry_ref[h])
        return (jnp.max(top) > SB_DEAD).astype(jnp.int32)

    for h in range(n_heads):
        block(h, i, True)

    def body(c):
        jj, _ = c
        for h in range(n_heads):
            block(h, i - 1 - jj, False)
        return jj + 1, any_live()

    lax.while_loop(lambda c: jnp.logical_and(c[0] < i, c[1] > 0), body, (0, any_live()))
    for h in range(n_heads):
        cs = slice(h * LANES, (h + 1) * LANES)
        o_ref[:, cs] = (acc_ref[h] * _silu(g_ref[:, cs])).astype(o_ref.dtype)


def _sb_prompt(p, *, batch, n_heads, tb, cq, ck, cv, cg):
    grid, in_specs, out_specs, out_shape = _attn_specs(
        p, batch=batch, n_heads=n_heads, tb=tb, cq=cq, ck=ck, cv=cv, cg=cg)
    return pl.pallas_call(
        functools.partial(_sb_prompt_kernel, tb=tb, n_heads=n_heads, scale=LANES ** -0.5),
        grid=grid,
        in_specs=in_specs,
        out_specs=out_specs,
        out_shape=out_shape,
        scratch_shapes=[
            pltpu.VMEM((n_heads, tb, LANES), BF16),
            pltpu.VMEM((n_heads, tb, LANES), F32),
            pltpu.VMEM((n_heads, tb, LANES), F32),
        ],
        compiler_params=pltpu.CompilerParams(dimension_semantics=("arbitrary", "arbitrary")),
        name="sb_prompt",
    )(*([p] * (2 + 2 * n_heads)))


def _fox_prompt_kernel(q_ref, g_ref, *refs, tb, n_heads, scale):
    k_refs = refs[:n_heads]
    v_refs = refs[n_heads:2 * n_heads]
    fq_ref, ft_ref, o_ref, ko_ref, vo_ref, qs_ref, acc_ref, m_ref, l_ref = refs[2 * n_heads:]
    i = pl.program_id(1)
    t = k_refs[0].shape[0]

    @pl.when(i == 0)
    def _():
        _interleave_rows(ko_ref, k_refs, t, tb)
        _interleave_rows(vo_ref, v_refs, t, tb)

    valid = _iota((tb, tb), 1) <= _iota((tb, tb), 0)
    ones = jnp.ones((tb, LANES), BF16)
    wide = lambda a: jnp.concatenate([a] * (tb // LANES), axis=1)
    for h in range(n_heads):
        qs_ref[h] = (q_ref[:, h * LANES:(h + 1) * LANES] * scale).astype(BF16)
    acc_ref[...] = jnp.zeros_like(acc_ref)
    l_ref[...] = jnp.zeros_like(l_ref)
    m_ref[...] = jnp.full_like(m_ref, NEG_BIG)

    def block(h, j, diagonal):
        off = pl.multiple_of(j * tb, tb)
        k = k_refs[h][pl.ds(off, tb), :].astype(BF16)
        v1 = jnp.concatenate([v_refs[h][pl.ds(off, tb), :].astype(BF16), ones], axis=1)
        fq = fq_ref[:, h * LANES:(h + 1) * LANES]
        fk = ft_ref[h, :, pl.ds(off, tb)]
        z = _dot_nt(qs_ref[h], k) + (wide(fq) - fk)
        if diagonal:
            z = jnp.where(valid, z, NEG_BIG)
        m_old = m_ref[h]
        m_new = jnp.maximum(m_old, jnp.max(z, axis=1, keepdims=True))
        alpha = jnp.exp(m_old - m_new)
        pv = _dot(jnp.exp(z - wide(m_new)).astype(BF16), v1)
        acc_ref[h] = alpha * acc_ref[h] + pv[:, :LANES]
        l_ref[h] = alpha * l_ref[h] + pv[:, LANES:]
        m_ref[h] = m_new

    for h in range(n_heads):
        block(h, i, True)

    def body(jj, c):
        for h in range(n_heads):
            block(h, jj, False)
        return c

    lax.fori_loop(0, i, body, 0)
    for h in range(n_heads):
        cs = slice(h * LANES, (h + 1) * LANES)
        o_ref[:, cs] = ((acc_ref[h] / l_ref[h]) * _silu(g_ref[:, cs])).astype(o_ref.dtype)


def _fox_prompt(p, fq, ft, *, batch, n_heads, tb, cq, ck, cv, cg):
    grid, in_specs, out_specs, out_shape = _attn_specs(
        p, batch=batch, n_heads=n_heads, tb=tb, cq=cq, ck=ck, cv=cv, cg=cg)
    m = p.shape[0]
    t = m // batch
    nq = t // tb
    in_specs += [
        pl.BlockSpec((tb, n_heads * LANES), lambda b, i: (b * nq + i, 0)),
        pl.BlockSpec((None, n_heads, 1, t), lambda b, i: (b, 0, 0, 0)),
    ]
    return pl.pallas_call(
        functools.partial(_fox_prompt_kernel, tb=tb, n_heads=n_heads, scale=LANES ** -0.5),
        grid=grid,
        in_specs=in_specs,
        out_specs=out_specs,
        out_shape=out_shape,
        scratch_shapes=[
            pltpu.VMEM((n_heads, tb, LANES), BF16),
            pltpu.VMEM((n_heads, tb, LANES), F32),
            pltpu.VMEM((n_heads, tb, LANES), F32),
            pltpu.VMEM((n_heads, tb, LANES), F32),
        ],
        compiler_params=pltpu.CompilerParams(dimension_semantics=("arbitrary", "arbitrary")),
        name="fox_prompt",
    )(*([p] * (2 + 2 * n_heads)), fq, ft)


def _gla_prompt_kernel(l_ref, q_ref, k_ref, v_ref, g_ref, la_ref, ng_ref, y_ref, s_ref, st_ref, *,
                       tg, n_pairs, scale):
    c_len = GLA_CHUNK
    half = LANES // 2
    ti = pl.program_id(1)

    @pl.when(ti == 0)
    def _():
        st_ref[...] = jnp.zeros_like(st_ref)

    lower = (_iota((c_len, c_len), 0) >= _iota((c_len, c_len), 1)).astype(BF16)
    lane = _iota((c_len, LANES), 1)
    first = lane < half
    r2 = _iota((2 * c_len, 2 * c_len), 0)
    c2 = _iota((2 * c_len, 2 * c_len), 1)
    sh = c_len.bit_length() - 1
    att_mask = jnp.logical_and((r2 >> sh) == (c2 >> sh), (c2 & (c_len - 1)) <= (r2 & (c_len - 1)))
    lane_sq = _iota((LANES, LANES), 1) < half
    ng = ng_ref[...]

    for c in range(tg // c_len):
        rows = slice(c * c_len, (c + 1) * c_len)
        la = la_ref[rows, :]
        b = _dot_split(la, lower, 3, left=True)
        b_last = b[c_len - 1:c_len, :]
        qd = q_ref[rows, :] * scale * jnp.exp(b)
        kk = k_ref[rows, :]
        kd = kk * jnp.exp(-b)
        ku = kk * jnp.exp(b_last - b)
        e_last = jnp.exp(b_last)
        for pr in range(n_pairs):
            ls = slice(pr * LANES, (pr + 1) * LANES)
            qd_p = qd[:, ls]
            qs = jnp.concatenate([jnp.where(first, qd_p, 0.0), jnp.where(first, 0.0, qd_p)],
                                 axis=0).astype(BF16)
            kd_p = kd[:, ls].astype(BF16)
            kd2 = jnp.concatenate([kd_p, kd_p], axis=0)
            att = jnp.where(att_mask, _dot_nt(qs, kd2), 0.0).astype(BF16)
            v_pair = v_ref[rows, 2 * pr * LANES:(2 * pr + 2) * LANES].astype(BF16)
            v_stack = jnp.concatenate([v_pair[:, :LANES], v_pair[:, LANES:]], axis=0)
            st = st_ref[pr]
            o = _dot(att, v_stack) + _dot_nt(qs, st.astype(BF16))
            upd = _dot_tn(v_pair, ku[:, ls].astype(BF16))
            st_ref[pr] = st * e_last[:, ls] + jnp.where(lane_sq, upd[:LANES], upd[LANES:])
            for s in range(2):
                hh = 2 * pr + s
                cs = slice(hh * LANES, (hh + 1) * LANES)
                oh = _rms(o[s * c_len:(s + 1) * c_len], ng)
                y_ref[rows, cs] = (oh * _silu(g_ref[rows, cs])).astype(y_ref.dtype)

    @pl.when(ti == pl.num_programs(1) - 1)
    def _():
        for pr in range(n_pairs):
            s_t = st_ref[pr].T
            s_ref[2 * pr] = s_t[:half]
            s_ref[2 * pr + 1] = s_t[half:]


def _gla_prompt(l_arr, p, la, ng, *, batch, n_heads, dk, dv, tg, cq, ck, cv, cg):
    m = p.shape[0]
    t = m // batch
    nt = t // tg
    wk = n_heads * dk
    wv = n_heads * dv
    n_pairs = n_heads // 2
    return pl.pallas_call(
        functools.partial(_gla_prompt_kernel, tg=tg, n_pairs=n_pairs, scale=dk ** -0.5),
        grid_spec=pltpu.PrefetchScalarGridSpec(
            num_scalar_prefetch=1,
            grid=(batch, nt),
            in_specs=[
                pl.BlockSpec((tg, wk), lambda b, i, l: (b * nt + i, cq)),
                pl.BlockSpec((tg, wk), lambda b, i, l: (b * nt + i, ck)),
                pl.BlockSpec((tg, wv), lambda b, i, l: (b * nt + i, cv)),
                pl.BlockSpec((tg, wv), lambda b, i, l: (b * nt + i, cg)),
                pl.BlockSpec((tg, wk), lambda b, i, l: (b * nt + i, 0)),
                pl.BlockSpec((None, 1, dv), lambda b, i, l: (l[0], 0, 0)),
            ],
            out_specs=[
                pl.BlockSpec((tg, wv), lambda b, i, l: (b * nt + i, 0)),
                pl.BlockSpec((None, n_heads, dk, dv), lambda b, i, l: (b, 0, 0, 0)),
            ],
            scratch_shapes=[pltpu.VMEM((n_pairs, LANES, LANES), F32)],
        ),
        out_shape=[jax.ShapeDtypeStruct((m, wv), BF16),
                   jax.ShapeDtypeStruct((batch, n_heads, dk, dv), F32)],
        compiler_params=pltpu.CompilerParams(dimension_semantics=("arbitrary", "arbitrary")),
        name="gla_prompt",
    )(l_arr, p, p, p, p, la, ng)


def _out_kernel(l_ref, x_ref, ya_ref, yb_ref, yc_ref, w0, w1, w2, w3, gain_ref, o_ref, *h_ref,
                final):
    wq = w0.shape[0]
    yb = yb_ref[...].astype(BF16)
    acc = x_ref[...] + _dot(ya_ref[...].astype(BF16), w0[...])
    acc = acc + _dot(yb[:, :wq], w1[...]) + _dot(yb[:, wq:], w2[...])
    acc = acc + _dot(yc_ref[...].astype(BF16), w3[...])
    if final:
        o_ref[...] = _rms(acc, gain_ref[...])
    else:
        o_ref[...] = acc
        h_ref[0][...] = _rms(acc, gain_ref[...]).astype(BF16)


def _out_proj(l_arr, x, ya, yb, yc, w_out, gain, *, tm, final):
    m, d = x.shape
    wq = ya.shape[1]
    assert yb.shape[1] == 2 * wq and yc.shape[1] == wq and w_out.shape[1] == 4 * wq
    wspec = lambda r: pl.BlockSpec((None, wq, d), lambda i, l, r=r: (l[0], r, 0))
    row_spec = pl.BlockSpec((tm, d), lambda i, l: (i, 0))
    out_specs = row_spec if final else [row_spec, row_spec]
    out_shape = jax.ShapeDtypeStruct((m, d), F32)
    if not final:
        out_shape = [out_shape, jax.ShapeDtypeStruct((m, d), BF16)]
    return pl.pallas_call(
        functools.partial(_out_kernel, final=final),
        grid_spec=pltpu.PrefetchScalarGridSpec(
            num_scalar_prefetch=1,
            grid=(m // tm,),
            in_specs=[
                pl.BlockSpec((tm, d), lambda i, l: (i, 0)),
                pl.BlockSpec((tm, wq), lambda i, l: (i, 0)),
                pl.BlockSpec((tm, 2 * wq), lambda i, l: (i, 0)),
                pl.BlockSpec((tm, wq), lambda i, l: (i, 0)),
                wspec(0), wspec(1), wspec(2), wspec(3),
                pl.BlockSpec((1, d), lambda i, l: (0, 0)),
            ],
            out_specs=out_specs,
        ),
        out_shape=out_shape,
        compiler_params=pltpu.CompilerParams(dimension_semantics=("arbitrary",)),
        name="out_proj",
    )(l_arr, x, ya, yb, yc, w_out, w_out, w_out, w_out, gain)


def _rev_excl_cumsum(x, later, parts):
    r, n = x.shape
    nb = n // LANES
    xs = jnp.concatenate([x[:, i * LANES:(i + 1) * LANES] for i in range(nb)], axis=0)
    cs = _dot_split(xs, later, parts)
    tot = jnp.sum(xs, axis=1, keepdims=True)
    run = jnp.zeros((r, 1), F32)
    blocks = [None] * nb
    for i in range(nb - 1, -1, -1):
        blocks[i] = cs[i * r:(i + 1) * r] + run
        run = run + tot[i * r:(i + 1) * r]
    return jnp.concatenate(blocks, axis=1), run


def _head_rows(row_vec, h):
    part = jnp.broadcast_to(row_vec[:, h * LANES:(h + 1) * LANES], (SUBLANES, LANES))
    return jnp.where(_iota((SUBLANES, LANES), 0) == h, part, 0.0)


def _head_block(buf, slot, h, tok, n_heads):
    return buf[slot, pl.ds(h, tok, stride=n_heads), :].astype(BF16)


def _paged_scores(q_row, kbuf, slot, tok, n_heads):
    z = None
    for h in range(n_heads):
        t = _dot_nt(_head_rows(q_row, h).astype(BF16), _head_block(kbuf, slot, h, tok, n_heads))
        z = t if z is None else z + t
    return z


def _own_rows(acc_ref, n_heads, denom=None):
    parts = []
    for h in range(n_heads):
        a = acc_ref[h] if denom is None else acc_ref[h] / denom
        parts.append(a[h:h + 1, :])
    return jnp.concatenate(parts, axis=1)


def _page_copies(pt_ref, l, bb, gg, slot, srcs, bufs, sem, *, n_groups, g_pages):
    base = (n_groups - 1 - gg) * g_pages
    out = []
    for r in range(g_pages):
        pg = pt_ref[bb, base + r]
        for si, (src, buf) in enumerate(zip(srcs, bufs)):
            rows = src.shape[2]
            out.append(pltpu.make_async_copy(
                src.at[l, pg], buf.at[slot, pl.ds(r * rows, rows)], sem.at[si, slot]))
    return out


def _paged_pipeline(pt_ref, l, srcs, bufs, sem, *, n_groups, g_pages):
    bb = pl.program_id(0)
    gg = pl.program_id(1)
    step = bb * n_groups + gg
    total = pl.num_programs(0) * n_groups
    slot = lax.rem(step, 2)
    mk = functools.partial(_page_copies, pt_ref, l, srcs=srcs, bufs=bufs, sem=sem,
                           n_groups=n_groups, g_pages=g_pages)

    @pl.when(step == 0)
    def _():
        for cp in mk(bb, gg, slot):
            cp.start()

    @pl.when(step + 1 < total)
    def _():
        wrap = gg + 1 == n_groups
        nb = jnp.where(wrap, bb + 1, bb)
        ng = jnp.where(wrap, 0, gg + 1)
        for cp in mk(nb, ng, 1 - slot):
            cp.start()

    for cp in mk(bb, gg, slot):
        cp.wait()
    return slot


def _sb_decode_kernel(pt_ref, l_ref, p_ref, kc_ref, vc_ref, o_ref, kbuf, vbuf, sem, acc_ref,
                      carry_ref, flag_ref, *, n_groups, g_pages, tok, n_heads, scale, cq, cg):
    bb = pl.program_id(0)
    gg = pl.program_id(1)
    w = n_heads * LANES
    step = bb * n_groups + gg
    slot = lax.rem(step, 2)
    mk = functools.partial(_page_copies, pt_ref, l_ref[0], srcs=(kc_ref, vc_ref),
                           bufs=(kbuf, vbuf), sem=sem, n_groups=n_groups, g_pages=g_pages)

    @pl.when(step == 0)
    def _():
        for cp in mk(bb, gg, slot):
            cp.start()
        flag_ref[1] = 1

    @pl.when(gg == 0)
    def _():
        acc_ref[...] = jnp.zeros_like(acc_ref)
        carry_ref[...] = jnp.zeros_like(carry_ref)
        flag_ref[0] = 1

    live = flag_ref[0] > 0
    requested = flag_ref[1] > 0
    last_group = gg + 1 == n_groups
    want_next = jnp.where(last_group, bb + 1 < pl.num_programs(0), live)

    @pl.when(want_next)
    def _():
        nb = jnp.where(last_group, bb + 1, bb)
        ng = jnp.where(last_group, 0, gg + 1)
        for cp in mk(nb, ng, 1 - slot):
            cp.start()

    @pl.when(requested)
    def _():
        for cp in mk(bb, gg, slot):
            cp.wait()

    @pl.when(jnp.logical_and(requested, live))
    def _():
        q_row = p_ref[:, cq * LANES:cq * LANES + w] * scale
        later = (_iota((LANES, LANES), 0) > _iota((LANES, LANES), 1)).astype(BF16)
        z = _paged_scores(q_row, kbuf, slot, tok, n_heads)
        l1m = -_softplus(z)
        cum, tot = _rev_excl_cumsum(l1m, later, 2)
        wgt = jnp.exp(z + l1m + cum + carry_ref[...]).astype(BF16)
        for h in range(n_heads):
            acc_ref[h] += _dot(wgt, _head_block(vbuf, slot, h, tok, n_heads))
        carry = carry_ref[...] + tot
        carry_ref[...] = carry
        heads = _iota(carry.shape, 0) < n_heads
        flag_ref[0] = (jnp.max(jnp.where(heads, carry, NEG_BIG)) > SB_DEAD).astype(jnp.int32)

    flag_ref[1] = want_next.astype(jnp.int32)

    @pl.when(gg == n_groups - 1)
    def _():
        gate = p_ref[:, cg * LANES:cg * LANES + w]
        o_ref[...] = _own_rows(acc_ref, n_heads) * _silu(gate)


def _sb_decode(page_table, l_arr, p, kc, vc, *, n_heads, g_pages, cq, cg):
    n_seq, n_pages = page_table.shape
    rows = kc.shape[2]
    w = n_heads * LANES
    n_groups = n_pages // g_pages
    tok = g_pages * rows // n_heads
    return pl.pallas_call(
        functools.partial(_sb_decode_kernel, n_groups=n_groups, g_pages=g_pages, tok=tok,
                          n_heads=n_heads, scale=LANES ** -0.5, cq=cq, cg=cg),
        grid_spec=pltpu.PrefetchScalarGridSpec(
            num_scalar_prefetch=2,
            grid=(n_seq, n_groups),
            in_specs=[
                pl.BlockSpec((None, 1, p.shape[2]), lambda b, g, pt, l: (b, 0, 0)),
                pl.BlockSpec(memory_space=pl.ANY),
                pl.BlockSpec(memory_space=pl.ANY),
            ],
            out_specs=pl.BlockSpec((None, 1, w), lambda b, g, pt, l: (b, 0, 0)),
            scratch_shapes=[
                pltpu.VMEM((2, g_pages * rows, LANES), F32),
                pltpu.VMEM((2, g_pages * rows, LANES), F32),
                pltpu.SemaphoreType.DMA((2, 2)),
                pltpu.VMEM((n_heads, SUBLANES, LANES), F32),
                pltpu.VMEM((SUBLANES, 1), F32),
                pltpu.SMEM((2,), jnp.int32),
            ],
        ),
        out_shape=jax.ShapeDtypeStruct((n_seq, 1, w), F32),
        compiler_params=pltpu.CompilerParams(dimension_semantics=("arbitrary", "arbitrary")),
        name="sb_decode",
    )(page_table, l_arr, p, kc, vc)


def _fox_decode_kernel(pt_ref, l_ref, p_ref, lfn_ref, kc_ref, vc_ref, lc_ref, o_ref, kbuf, vbuf,
                       lbuf, sem, acc_ref, m_ref, l_sum_ref, carry_ref, *, n_groups, g_pages, tok,
                       n_heads, scale, cq, ck, cv, cg):
    gg = pl.program_id(1)
    w = n_heads * LANES
    slot = _paged_pipeline(pt_ref, l_ref[0], (kc_ref, vc_ref, lc_ref), (kbuf, vbuf, lbuf), sem,
                           n_groups=n_groups, g_pages=g_pages)
    q_row = p_ref[:, cq * LANES:cq * LANES + w] * scale

    @pl.when(gg == 0)
    def _():
        k_new = p_ref[:, ck * LANES:ck * LANES + w]
        v_new = p_ref[:, cv * LANES:cv * LANES + w]
        z_new = jnp.zeros((SUBLANES, 1), F32)
        for h in range(n_heads):
            cs = slice(h * LANES, (h + 1) * LANES)
            z_new = z_new + jnp.sum(_head_rows(q_row, h) * k_new[:, cs], axis=1, keepdims=True)
            acc_ref[h] = jnp.broadcast_to(v_new[:, cs], (SUBLANES, LANES))
        m_ref[...] = z_new
        l_sum_ref[...] = jnp.ones_like(l_sum_ref)
        pick = _iota((SUBLANES, LANES), 1) == _iota((SUBLANES, LANES), 0) + SIDE_FC
        lf_row = jnp.broadcast_to(lfn_ref[...], (SUBLANES, LANES))
        carry_ref[...] = jnp.sum(jnp.where(pick, lf_row, 0.0), axis=1, keepdims=True)

    later = (_iota((LANES, LANES), 0) > _iota((LANES, LANES), 1)).astype(BF16)
    lf = jnp.concatenate([lbuf[slot, pl.ds(r * SUBLANES, SUBLANES), :] for r in range(g_pages)],
                         axis=1)
    bias, tot = _rev_excl_cumsum(lf, later, 3)
    z = _paged_scores(q_row, kbuf, slot, tok, n_heads) + (bias + carry_ref[...])
    m_old = m_ref[...]
    m_new = jnp.maximum(m_old, jnp.max(z, axis=1, keepdims=True))
    alpha = jnp.exp(m_old - m_new)
    pr = jnp.exp(z - m_new)
    l_sum_ref[...] = alpha * l_sum_ref[...] + jnp.sum(pr, axis=1, keepdims=True)
    pr = pr.astype(BF16)
    for h in range(n_heads):
        acc_ref[h] = alpha * acc_ref[h] + _dot(pr, _head_block(vbuf, slot, h, tok, n_heads))
    m_ref[...] = m_new
    carry_ref[...] += tot

    @pl.when(gg == n_groups - 1)
    def _():
        gate = p_ref[:, cg * LANES:cg * LANES + w]
        o_ref[...] = _own_rows(acc_ref, n_heads, l_sum_ref[...]) * _silu(gate)


def _fox_decode(page_table, l_arr, p, lf_new, kc, vc, lc, *, n_heads, g_pages, cq, ck, cv, cg):
    n_seq, n_pages = page_table.shape
    rows = kc.shape[2]
    w = n_heads * LANES
    n_groups = n_pages // g_pages
    tok = g_pages * rows // n_heads
    return pl.pallas_call(
        functools.partial(_fox_decode_kernel, n_groups=n_groups, g_pages=g_pages, tok=tok,
                          n_heads=n_heads, scale=LANES ** -0.5, cq=cq, ck=ck, cv=cv, cg=cg),
        grid_spec=pltpu.PrefetchScalarGridSpec(
            num_scalar_prefetch=2,
            grid=(n_seq, n_groups),
            in_specs=[
                pl.BlockSpec((None, 1, p.shape[2]), lambda b, g, pt, l: (b, 0, 0)),
                pl.BlockSpec((None, 1, lf_new.shape[2]), lambda b, g, pt, l: (b, 0, 0)),
                pl.BlockSpec(memory_space=pl.ANY),
                pl.BlockSpec(memory_space=pl.ANY),
                pl.BlockSpec(memory_space=pl.ANY),
            ],
            out_specs=pl.BlockSpec((None, 1, w), lambda b, g, pt, l: (b, 0, 0)),
            scratch_shapes=[
                pltpu.VMEM((2, g_pages * rows, LANES), F32),
                pltpu.VMEM((2, g_pages * rows, LANES), F32),
                pltpu.VMEM((2, g_pages * SUBLANES, LANES), F32),
                pltpu.SemaphoreType.DMA((3, 2)),
                pltpu.VMEM((n_heads, SUBLANES, LANES), F32),
                pltpu.VMEM((SUBLANES, 1), F32),
                pltpu.VMEM((SUBLANES, 1), F32),
                pltpu.VMEM((SUBLANES, 1), F32),
            ],
        ),
        out_shape=jax.ShapeDtypeStruct((n_seq, 1, w), F32),
        compiler_params=pltpu.CompilerParams(dimension_semantics=("arbitrary", "arbitrary")),
        name="fox_decode",
    )(page_table, l_arr, p, lf_new, kc, vc, lc)


def _gla_decode_kernel(l_ref, p_ref, la_ref, s0_ref, ng_ref, y_ref, s_ref, *, n_heads, dk, dv,
                       scale, cq, ck, cv, cg):
    bb = pl.program_id(0)
    wk = n_heads * dk

    q_row = p_ref[:,cq * LANES:cq * LANES + wk]
    k_row = p_ref[:,ck * LANES:ck * LANES + wk]
    e_row = jnp.exp(la_ref[...])
    lane = _iota((dk, wk), 1)
    sub = _iota((dk, wk), 0)
    ng = ng_ref[...]

    def column(row_vec, hh):
        pick = lane == sub + hh * dk
        return jnp.sum(jnp.where(pick, jnp.broadcast_to(row_vec, (dk, wk)), 0.0),
                       axis=1, keepdims=True)

    for hh in range(n_heads):
        cs = slice(hh * dv, (hh + 1) * dv)
        v_row = p_ref[:,cv * LANES + hh * dv:cv * LANES + (hh + 1) * dv]
        s_new = column(e_row, hh) * s0_ref[hh] + column(k_row, hh) * v_row
        s_ref[hh] = s_new
        o = jnp.sum((column(q_row, hh) * scale) * s_new, axis=0, keepdims=True)
        gate = p_ref[:,cg * LANES + hh * dv:cg * LANES + (hh + 1) * dv]
        y_ref[:, cs] = _rms(o, ng) * _silu(gate)


def _gla_decode(l_arr, p, la, state, ng, *, cq, ck, cv, cg):
    _, n_seq, n_heads, dk, dv = state.shape
    return pl.pallas_call(
        functools.partial(_gla_decode_kernel, n_heads=n_heads, dk=dk, dv=dv, scale=dk ** -0.5,
                          cq=cq, ck=ck, cv=cv, cg=cg),
        grid_spec=pltpu.PrefetchScalarGridSpec(
            num_scalar_prefetch=1,
            grid=(n_seq,),
            in_specs=[
                pl.BlockSpec((None, 1, p.shape[2]), lambda b, l: (b, 0, 0)),
                pl.BlockSpec((None, 1, la.shape[2]), lambda b, l: (b, 0, 0)),
                pl.BlockSpec((None, None, n_heads, dk, dv), lambda b, l: (l[0], b, 0, 0, 0)),
                pl.BlockSpec((None, 1, dv), lambda b, l: (l[0], 0, 0)),
            ],
            out_specs=[
                pl.BlockSpec((None, 1, n_heads * dv), lambda b, l: (b, 0, 0)),
                pl.BlockSpec((None, n_heads, dk, dv), lambda b, l: (b, 0, 0, 0)),
            ],
        ),
        out_shape=[jax.ShapeDtypeStruct((n_seq, 1, n_heads * dv), F32),
                   jax.ShapeDtypeStruct((n_seq, n_heads, dk, dv), F32)],
        compiler_params=pltpu.CompilerParams(dimension_semantics=("arbitrary",)),
        name="gla_decode",
    )(l_arr, p, la, state, ng)


DEC_ROWS = 16
PROJ_TM = 1024
SIDE_TA = 256
ATT_TB = 256
GLA_TG = 256
OUT_TM = 512
PAGES_PER_STEP = 16
SB_PAGES_PER_STEP = 8


def kernel(x_prompt, x_sample, cache_sb_k, cache_sb_v, cache_fox_k, cache_fox_v, cache_fox_logf,
           state_gla, page_table, norm_g, w_in, w_gate_b, b_gate, b_forget, q_norm_g, k_norm_g,
           gla_norm_g, w_out, norm_f):
    batch, seq, d = x_prompt.shape
    n_seq = x_sample.shape[0]
    depth, n_pool, page, h_sb, hd = cache_sb_k.shape
    h_fox = cache_fox_k.shape[3]
    _, _, h_gla, dk, dv = state_gla.shape
    rank = w_gate_b.shape[1]
    assert hd == LANES and dv == LANES and 2 * dk == LANES and x_sample.shape[1] == 1
    w_sb, w_fox, w_gk, w_gv = h_sb * hd, h_fox * hd, h_gla * dk, h_gla * dv
    assert w_sb == w_fox == w_gk and w_gv == 2 * w_sb

    sizes = (w_sb,) * 4 + (w_gk, w_gk, w_gv, w_gv, rank) + (w_fox,) * 4 + (h_fox,)
    offs = [0]
    for s in sizes:
        offs.append(offs[-1] + s)
    assert offs[-1] == w_in.shape[2]
    o_ab, o_qc, o_fc = offs[8], offs[9], offs[13]
    n_main = o_ab + (o_fc - o_qc)
    c_qa, c_ka, c_va, c_ga = (offs[i] // LANES for i in range(4))
    c_qb, c_kb, c_vb, c_gb = (offs[i] // LANES for i in range(4, 8))
    c_qc, c_kc, c_vc, c_gc = ((offs[i] - rank) // LANES for i in range(9, 13))

    w_b = w_in[:, :, o_qc:o_fc].astype(BF16)
    w_side = jnp.zeros((depth, d, LANES), F32)
    w_side = w_side.at[:, :, SIDE_FC:SIDE_FC + h_fox].set(w_in[:, :, o_fc:])
    w_side = w_side.at[:, :, SIDE_AB:SIDE_AB + rank].set(w_in[:, :, o_ab:o_qc]).astype(BF16)
    wgb = jnp.zeros((depth, LANES, w_gk), F32).at[:, SIDE_AB:SIDE_AB + rank].set(w_gate_b)
    wgb = wgb.astype(BF16)
    bg = b_gate.reshape(depth, 1, w_gk)
    bfg = jnp.zeros((depth, 1, LANES), F32).at[:, 0, SIDE_FC:SIDE_FC + h_fox].set(b_forget)
    w_o = w_out.astype(BF16)
    ng3 = norm_g.reshape(depth, 1, d)
    qg3 = q_norm_g.reshape(depth, 1, hd)
    kg3 = k_norm_g.reshape(depth, 1, hd)
    gg3 = gla_norm_g.reshape(depth, 1, dv)
    nf2 = norm_f.reshape(1, d)

    kc_sb = cache_sb_k.reshape(depth, n_pool, page * h_sb, hd)
    vc_sb = cache_sb_v.reshape(depth, n_pool, page * h_sb, hd)
    kc_fx = cache_fox_k.reshape(depth, n_pool, page * h_fox, hd)
    vc_fx = cache_fox_v.reshape(depth, n_pool, page * h_fox, hd)
    lc_fx = jnp.pad(jnp.swapaxes(cache_fox_logf, 2, 3), ((0, 0), (0, 0), (0, SUBLANES - h_fox), (0, 0)))

    m_p = batch * seq
    hp = x_prompt.reshape(m_p, d)
    hs = jnp.pad(x_sample.reshape(n_seq, d), ((0, DEC_ROWS - n_seq), (0, 0)))

    tn = 2 * w_fox
    assert o_ab % tn == 0 and (o_fc - o_qc) % tn == 0
    proj = functools.partial(_proj, tn=tn, ja=o_ab // tn)

    l0 = jnp.zeros((1,), jnp.int32)
    hp_n = _norm(l0, hp, ng3, tm=min(OUT_TM, m_p))
    hs_n = _norm(l0, hs, ng3, tm=DEC_ROWS)
    outs = [[] for _ in range(12)]
    for layer in range(depth):
        l_arr = jnp.full((1,), layer, jnp.int32)
        final = layer == depth - 1
        gain = nf2 if final else ng3[layer + 1]

        p = proj(l_arr, hp_n, w_in, w_b, qg3, kg3, tm=min(PROJ_TM, m_p))
        la, lf, fq, ft = _side(l_arr, hp_n, w_side, wgb, bg, bfg, batch=batch,
                               ta=min(SIDE_TA, seq), n_heads=h_fox, with_cumsum=True)
        tb = min(ATT_TB, seq)
        ya, ka_o, va_o = _sb_prompt(p, batch=batch, n_heads=h_sb, tb=tb,
                                    cq=c_qa, ck=c_ka, cv=c_va, cg=c_ga)
        yb, s_fin = _gla_prompt(l_arr, p, la, gg3, batch=batch, n_heads=h_gla, dk=dk, dv=dv,
                                tg=min(GLA_TG, seq), cq=c_qb * LANES // w_gk, ck=c_kb * LANES // w_gk,
                                cv=c_vb * LANES // w_gv, cg=c_gb * LANES // w_gv)
        yc, kc_o, vc_o = _fox_prompt(p, fq, ft, batch=batch, n_heads=h_fox, tb=tb,
                         cq=c_qc, ck=c_kc, cv=c_vc, cg=c_gc)
        hp = _out_proj(l_arr, hp, ya, yb, yc, w_o, gain, tm=min(OUT_TM, m_p), final=final)
        if not final:
            hp, hp_n = hp
        outs[0].append(ka_o.reshape(batch, seq, h_sb, hd))
        outs[1].append(va_o.reshape(batch, seq, h_sb, hd))
        outs[2].append(kc_o.reshape(batch, seq, h_fox, hd))
        outs[3].append(vc_o.reshape(batch, seq, h_fox, hd))
        outs[4].append(lf[:, SIDE_FC:SIDE_FC + h_fox].reshape(batch, seq, h_fox))
        outs[5].append(s_fin)

        ps = proj(l_arr, hs_n, w_in, w_b, qg3, kg3, tm=DEC_ROWS)
        las, lfs = _side(l_arr, hs_n, w_side, wgb, bg, bfg, batch=1, ta=DEC_ROWS,
                         n_heads=h_fox, with_cumsum=False)
        ps3 = ps.reshape(DEC_ROWS, 1, -1)
        g_pages = min(PAGES_PER_STEP, page_table.shape[1])
        yas = _sb_decode(page_table, l_arr, ps3, kc_sb, vc_sb, n_heads=h_sb,
                         g_pages=min(SB_PAGES_PER_STEP, page_table.shape[1]), cq=c_qa, cg=c_ga)
        ybs, s_new = _gla_decode(l_arr, ps3, las.reshape(DEC_ROWS, 1, -1), state_gla, gg3,
                                 cq=c_qb, ck=c_kb, cv=c_vb, cg=c_gb)
        ycs = _fox_decode(page_table, l_arr, ps3, lfs.reshape(DEC_ROWS, 1, -1), kc_fx, vc_fx, lc_fx,
                          n_heads=h_fox, g_pages=g_pages, cq=c_qc, ck=c_kc, cv=c_vc, cg=c_gc)
        pad_rows = lambda y: jnp.pad(y.reshape(n_seq, -1), ((0, DEC_ROWS - n_seq), (0, 0)))
        hs = _out_proj(l_arr, hs, pad_rows(yas), pad_rows(ybs), pad_rows(ycs), w_o, gain,
                       tm=DEC_ROWS, final=final)
        if not final:
            hs, hs_n = hs
        cols = lambda c, w: ps[:n_seq, c * LANES:c * LANES + w]
        outs[6].append(cols(c_ka, w_sb).reshape(n_seq, 1, h_sb, hd))
        outs[7].append(cols(c_va, w_sb).reshape(n_seq, 1, h_sb, hd))
        outs[8].append(cols(c_kc, w_fox).reshape(n_seq, 1, h_fox, hd))
        outs[9].append(cols(c_vc, w_fox).reshape(n_seq, 1, h_fox, hd))
        outs[10].append(lfs[:n_seq, SIDE_FC:SIDE_FC + h_fox].reshape(n_seq, 1, h_fox))
        outs[11].append(s_new)

    y_prompt = hp.reshape(batch, seq, d)
    y_sample = hs[:n_seq].reshape(n_seq, 1, d)
    return (y_prompt, y_sample) + tuple(jnp.stack(o) for o in outs)
```

```python
import functools

import jax
import jax.numpy as jnp
from jax import lax
from jax.experimental import pallas as pl
from jax.experimental.pallas import tpu as pltpu

F32 = jnp.float32
BF16 = jnp.bfloat16

LANES = 128
SUBLANES = 8
RMS_EPS = 1e-6
GATE_TAU = 16.0
GLA_CHUNK = 64
NEG_BIG = -1e30
SB_DEAD = -110.0

SIDE_FC = 16
SIDE_AB = 0


def _dot(a, b):
    return jnp.dot(a, b, preferred_element_type=F32)


def _dot_nt(a, b):
    return lax.dot_general(a, b, (((1,), (1,)), ((), ())), preferred_element_type=F32)


def _dot_tn(a, b):
    return lax.dot_general(a, b, (((0,), (0,)), ((), ())), preferred_element_type=F32)


def _softplus(z):
    return jnp.maximum(z, 0.0) + jnp.log(1.0 + jnp.exp(-jnp.abs(z)))


def _log_sigmoid(z):
    return -_softplus(-z)


def _silu(g):
    return g / (1.0 + jnp.exp(-g))


def _rms(x, g):
    ms = jnp.mean(x * x, axis=-1, keepdims=True)
    return x * lax.rsqrt(ms + RMS_EPS) * g


def _split_bf16(x, parts):
    out = []
    r = x
    for _ in range(parts):
        p = r.astype(BF16)
        out.append(p)
        r = r - p.astype(F32)
    return out


def _dot_split(x, m, parts, left=False):
    acc = None
    for p in _split_bf16(x, parts):
        t = _dot(m, p) if left else _dot(p, m)
        acc = t if acc is None else acc + t
    return acc


def _iota(shape, dim):
    return lax.broadcasted_iota(jnp.int32, shape, dim)


def _norm_kernel(l_ref, x_ref, g_ref, h_ref):
    h_ref[...] = _rms(x_ref[...], g_ref[...]).astype(BF16)


def _norm(l_arr, x, norm_g, *, tm):
    m, d = x.shape
    return pl.pallas_call(
        _norm_kernel,
        grid_spec=pltpu.PrefetchScalarGridSpec(
            num_scalar_prefetch=1,
            grid=(m // tm,),
            in_specs=[
                pl.BlockSpec((tm, d), lambda i, l: (i, 0)),
                pl.BlockSpec((None, 1, d), lambda i, l: (l[0], 0, 0)),
            ],
            out_specs=pl.BlockSpec((tm, d), lambda i, l: (i, 0)),
        ),
        out_shape=jax.ShapeDtypeStruct((m, d), BF16),
        compiler_params=pltpu.CompilerParams(dimension_semantics=("arbitrary",)),
        name="norm",
    )(l_arr, x, norm_g)


def _proj_kernel(l_ref, h_ref, wa_ref, wb_ref, qg_ref, kg_ref, o_ref, *, ja, jn):
    j = pl.program_id(0)

    @pl.when(j < ja)
    def _():
        o_ref[...] = _dot_nt(h_ref[...], wa_ref[...])

    @pl.when(jnp.logical_and(j >= ja, j != jn))
    def _():
        o_ref[...] = _dot_nt(h_ref[...], wb_ref[...])

    @pl.when(j == jn)
    def _():
        acc = _dot_nt(h_ref[...], wb_ref[...])
        n_h = acc.shape[1] // LANES
        for hh in range(n_h):
            cs = slice(hh * LANES, (hh + 1) * LANES)
            g = qg_ref[...] if hh < n_h // 2 else kg_ref[...]
            o_ref[:, cs] = _rms(acc[:, cs], g)


def _proj(l_arr, h, w_t, w_b, qg, kg, *, tm, tn, ja):
    m, d = h.shape
    jb = w_b.shape[1] // tn
    jn = ja
    return pl.pallas_call(
        functools.partial(_proj_kernel, ja=ja, jn=jn),
        grid_spec=pltpu.PrefetchScalarGridSpec(
            num_scalar_prefetch=1,
            grid=(ja + jb, m // tm),
            in_specs=[
                pl.BlockSpec((tm, d), lambda j, i, l: (i, 0)),
                pl.BlockSpec((None, tn, d), lambda j, i, l: (l[0], jnp.minimum(j, ja - 1), 0)),
                pl.BlockSpec((None, tn, d), lambda j, i, l: (l[0], jnp.maximum(j - ja, 0), 0)),
                pl.BlockSpec((None, 1, LANES), lambda j, i, l: (l[0], 0, 0)),
                pl.BlockSpec((None, 1, LANES), lambda j, i, l: (l[0], 0, 0)),
            ],
            out_specs=pl.BlockSpec((tm, tn), lambda j, i, l: (i, j)),
        ),
        out_shape=jax.ShapeDtypeStruct((m, (ja + jb) * tn), F32),
        compiler_params=pltpu.CompilerParams(dimension_semantics=("arbitrary", "arbitrary")),
        name="proj",
    )(l_arr, h, w_t, w_b, qg, kg)


def _side_kernel(l_ref, h_ref, ws_ref, wgb_ref, bg_ref, bf_ref, la_ref, lf_ref, *rest,
                 n_heads, with_cumsum):
    s = _dot_nt(h_ref[...], ws_ref[...])
    la_ref[...] = _log_sigmoid(_dot(s.astype(BF16), wgb_ref[...]) + bg_ref[...]) / GATE_TAU
    lf = _log_sigmoid(s + bf_ref[...])
    lf_ref[...] = lf
    if with_cumsum:
        fq_ref, ft_ref, carry_ref = rest
        ta = lf.shape[0]

        @pl.when(pl.program_id(1) == 0)
        def _():
            carry_ref[...] = jnp.zeros_like(carry_ref)

        lower = (_iota((ta, ta), 0) >= _iota((ta, ta), 1)).astype(BF16)
        f = _dot_split(lf, lower, 3, left=True) + carry_ref[...]
        carry_ref[...] = f[ta - 1:ta, :]
        ft = f.T
        for hh in range(n_heads):
            c = SIDE_FC + hh
            fq_ref[:, hh * LANES:(hh + 1) * LANES] = jnp.broadcast_to(f[:, c:c + 1], (ta, LANES))
            ft_ref[hh] = ft[c:c + 1, :]


def _side(l_arr, h, w_side, wgb, bg, bfg, *, batch, ta, n_heads, with_cumsum):
    m, d = h.shape
    t = m // batch
    nt = t // ta
    n_la = wgb.shape[2]
    in_specs = [
        pl.BlockSpec((ta, d), lambda b, i, l: (b * nt + i, 0)),
        pl.BlockSpec((None, LANES, d), lambda b, i, l: (l[0], 0, 0)),
        pl.BlockSpec((None, LANES, n_la), lambda b, i, l: (l[0], 0, 0)),
        pl.BlockSpec((None, 1, n_la), lambda b, i, l: (l[0], 0, 0)),
        pl.BlockSpec((None, 1, LANES), lambda b, i, l: (l[0], 0, 0)),
    ]
    out_specs = [
        pl.BlockSpec((ta, n_la), lambda b, i, l: (b * nt + i, 0)),
        pl.BlockSpec((ta, LANES), lambda b, i, l: (b * nt + i, 0)),
    ]
    out_shape = [jax.ShapeDtypeStruct((m, n_la), F32), jax.ShapeDtypeStruct((m, LANES), F32)]
    scratch = []
    if with_cumsum:
        out_specs += [
            pl.BlockSpec((ta, n_heads * LANES), lambda b, i, l: (b * nt + i, 0)),
            pl.BlockSpec((None, n_heads, 1, ta), lambda b, i, l: (b, 0, 0, i)),
        ]
        out_shape += [jax.ShapeDtypeStruct((m, n_heads * LANES), F32),
                      jax.ShapeDtypeStruct((batch, n_heads, 1, t), F32)]
        scratch = [pltpu.VMEM((1, LANES), F32)]
    return pl.pallas_call(
        functools.partial(_side_kernel, n_heads=n_heads, with_cumsum=with_cumsum),
        grid_spec=pltpu.PrefetchScalarGridSpec(
            num_scalar_prefetch=1, grid=(batch, nt), in_specs=in_specs, out_specs=out_specs,
            scratch_shapes=scratch),
        out_shape=out_shape,
        compiler_params=pltpu.CompilerParams(dimension_semantics=("arbitrary", "arbitrary")),
        name="side",
    )(l_arr, h, w_side, wgb, bg, bfg)


def _interleave_rows(dst_ref, src_refs, t, chunk):
    n = len(src_refs)

    def body(c, carry):
        off = pl.multiple_of(c * chunk, chunk)
        for h, src in enumerate(src_refs):
            dst_ref[pl.ds(off * n + h, chunk, stride=n), :] = src[pl.ds(off, chunk), :]
        return carry

    lax.fori_loop(0, t // chunk, body, 0)


def _attn_specs(p, *, batch, n_heads, tb, cq, ck, cv, cg):
    m = p.shape[0]
    t = m // batch
    nq = t // tb
    w = n_heads * LANES
    in_specs = [
        pl.BlockSpec((tb, w), lambda b, i: (b * nq + i, cq // n_heads)),
        pl.BlockSpec((tb, w), lambda b, i: (b * nq + i, cg // n_heads)),
    ]
    in_specs += [pl.BlockSpec((t, LANES), lambda b, i, c=ck + h: (b, c)) for h in range(n_heads)]
    in_specs += [pl.BlockSpec((t, LANES), lambda b, i, c=cv + h: (b, c)) for h in range(n_heads)]
    out_specs = [
        pl.BlockSpec((tb, w), lambda b, i: (b * nq + i, 0)),
        pl.BlockSpec((None, t * n_heads, LANES), lambda b, i: (b, 0, 0)),
        pl.BlockSpec((None, t * n_heads, LANES), lambda b, i: (b, 0, 0)),
    ]
    out_shape = [
        jax.ShapeDtypeStruct((m, w), BF16),
        jax.ShapeDtypeStruct((batch, t * n_heads, LANES), F32),
        jax.ShapeDtypeStruct((batch, t * n_heads, LANES), F32),
    ]
    return (batch, nq), in_specs, out_specs, out_shape


def _sb_prompt_kernel(q_ref, g_ref, *refs, tb, n_heads, scale):
    k_refs = refs[:n_heads]
    v_refs = refs[n_heads:2 * n_heads]
    o_ref, ko_ref, vo_ref, qs_ref, acc_ref, carry_ref = refs[2 * n_heads:]
    i = pl.program_id(1)
    t = k_refs[0].shape[0]

    @pl.when(i == 0)
    def _():
        _interleave_rows(ko_ref, k_refs, t, tb)
        _interleave_rows(vo_ref, v_refs, t, tb)

    row = _iota((2 * LANES, 2 * LANES), 0) & (LANES - 1)
    col = _iota((2 * LANES, 2 * LANES), 1)
    later2 = jnp.logical_or(row > col, col >= LANES).astype(BF16)
    valid = _iota((tb, tb), 1) < _iota((tb, tb), 0)
    for h in range(n_heads):
        qs_ref[h] = (q_ref[:, h * LANES:(h + 1) * LANES] * scale).astype(BF16)
    acc_ref[...] = jnp.zeros_like(acc_ref)
    carry_ref[...] = jnp.zeros_like(carry_ref)

    def block(h, j, diagonal):
        off = pl.multiple_of(j * tb, tb)
        k = k_refs[h][pl.ds(off, tb), :].astype(BF16)
        v = v_refs[h][pl.ds(off, tb), :].astype(BF16)
        z = _dot_nt(qs_ref[h], k)
        l1m = -_softplus(z)
        if diagonal:
            l1m = jnp.where(valid, l1m, 0.0)
        run = carry_ref[h]
        tiles = [None] * (tb // LANES)
        for c in range(tb // LANES - 1, -1, -1):
            ls = slice(c * LANES, (c + 1) * LANES)
            cum = _dot(jnp.concatenate(_split_bf16(l1m[:, ls], 2), axis=1), later2)
            tiles[c] = jnp.exp(z[:, ls] + l1m[:, ls] + (cum[:, :LANES] + run))
            run = run + cum[:, LANES:]
        w = jnp.concatenate(tiles, axis=1)
        if diagonal:
            w = jnp.where(valid, w, 0.0)
        acc_ref[h] += _dot(w.astype(BF16), v)
        carry_ref[h] = run

    def any_live():
        top = carry_ref[0]
        for h in range(1, n_heads):
            top = jnp.maximum(top, carry_ref[h])
        return (jnp.max(top) > SB_DEAD).astype(jnp.int32)

    for h in range(n_heads):
        block(h, i, True)

    def body(c):
        jj, _ = c
        for h in range(n_heads):
            block(h, i - 1 - jj, False)
        return jj + 1, any_live()

    lax.while_loop(lambda c: jnp.logical_and(c[0] < i, c[1] > 0), body, (0, any_live()))
    for h in range(n_heads):
        cs = slice(h * LANES, (h + 1) * LANES)
        o_ref[:, cs] = (acc_ref[h] * _silu(g_ref[:, cs])).astype(o_ref.dtype)


def _sb_prompt(p, *, batch, n_heads, tb, cq, ck, cv, cg):
    grid, in_specs, out_specs, out_shape = _attn_specs(
        p, batch=batch, n_heads=n_heads, tb=tb, cq=cq, ck=ck, cv=cv, cg=cg)
    return pl.pallas_call(
        functools.partial(_sb_prompt_kernel, tb=tb, n_heads=n_heads, scale=LANES ** -0.5),
        grid=grid,
        in_specs=in_specs,
        out_specs=out_specs,
        out_shape=out_shape,
        scratch_shapes=[
            pltpu.VMEM((n_heads, tb, LANES), BF16),
            pltpu.VMEM((n_heads, tb, LANES), F32),
            pltpu.VMEM((n_heads, tb, LANES), F32),
        ],
        compiler_params=pltpu.CompilerParams(dimension_semantics=("arbitrary", "arbitrary")),
        name="sb_prompt",
    )(*([p] * (2 + 2 * n_heads)))


def _fox_prompt_kernel(q_ref, g_ref, *refs, tb, n_heads, scale):
    k_refs = refs[:n_heads]
    v_refs = refs[n_heads:2 * n_heads]
    fq_ref, ft_ref, o_ref, ko_ref, vo_ref, qs_ref, acc_ref, m_ref, l_ref = refs[2 * n_heads:]
    i = pl.program_id(1)
    t = k_refs[0].shape[0]

    @pl.when(i == 0)
    def _():
        _interleave_rows(ko_ref, k_refs, t, tb)
        _interleave_rows(vo_ref, v_refs, t, tb)

    valid = _iota((tb, tb), 1) <= _iota((tb, tb), 0)
    ones = jnp.ones((tb, LANES), BF16)
    wide = lambda a: jnp.concatenate([a] * (tb // LANES), axis=1)
    for h in range(n_heads):
        qs_ref[h] = (q_ref[:, h * LANES:(h + 1) * LANES] * scale).astype(BF16)
    acc_ref[...] = jnp.zeros_like(acc_ref)
    l_ref[...] = jnp.zeros_like(l_ref)
    m_ref[...] = jnp.full_like(m_ref, NEG_BIG)

    def block(h, j, diagonal):
        off = pl.multiple_of(j * tb, tb)
        k = k_refs[h][pl.ds(off, tb), :].astype(BF16)
        v1 = jnp.concatenate([v_refs[h][pl.ds(off, tb), :].astype(BF16), ones], axis=1)
        fq = fq_ref[:, h * LANES:(h + 1) * LANES]
        fk = ft_ref[h, :, pl.ds(off, tb)]
        z = _dot_nt(qs_ref[h], k) + (wide(fq) - fk)
        if diagonal:
            z = jnp.where(valid, z, NEG_BIG)
        m_old = m_ref[h]
        m_new = jnp.maximum(m_old, jnp.max(z, axis=1, keepdims=True))
        alpha = jnp.exp(m_old - m_new)
        pv = _dot(jnp.exp(z - wide(m_new)).astype(BF16), v1)
        acc_ref[h] = alpha * acc_ref[h] + pv[:, :LANES]
        l_ref[h] = alpha * l_ref[h] + pv[:, LANES:]
        m_ref[h] = m_new

    for h in range(n_heads):
        block(h, i, True)

    def body(jj, c):
        for h in range(n_heads):
            block(h, jj, False)
        return c

    lax.fori_loop(0, i, body, 0)
    for h in range(n_heads):
        cs = slice(h * LANES, (h + 1) * LANES)
        o_ref[:, cs] = ((acc_ref[h] / l_ref[h]) * _silu(g_ref[:, cs])).astype(o_ref.dtype)


def _fox_prompt(p, fq, ft, *, batch, n_heads, tb, cq, ck, cv, cg):
    grid, in_specs, out_specs, out_shape = _attn_specs(
        p, batch=batch, n_heads=n_heads, tb=tb, cq=cq, ck=ck, cv=cv, cg=cg)
    m = p.shape[0]
    t = m // batch
    nq = t // tb
    in_specs += [
        pl.BlockSpec((tb, n_heads * LANES), lambda b, i: (b * nq + i, 0)),
        pl.BlockSpec((None, n_heads, 1, t), lambda b, i: (b, 0, 0, 0)),
    ]
    return pl.pallas_call(
        functools.partial(_fox_prompt_kernel, tb=tb, n_heads=n_heads, scale=LANES ** -0.5),
        grid=grid,
        in_specs=in_specs,
        out_specs=out_specs,
        out_shape=out_shape,
        scratch_shapes=[
            pltpu.VMEM((n_heads, tb, LANES), BF16),
            pltpu.VMEM((n_heads, tb, LANES), F32),
            pltpu.VMEM((n_heads, tb, LANES), F32),
            pltpu.VMEM((n_heads, tb, LANES), F32),
        ],
        compiler_params=pltpu.CompilerParams(dimension_semantics=("arbitrary", "arbitrary")),
        name="fox_prompt",
    )(*([p] * (2 + 2 * n_heads)), fq, ft)


def _gla_prompt_kernel(l_ref, q_ref, k_ref, v_ref, g_ref, la_ref, ng_ref, y_ref, s_ref, st_ref, *,
                       tg, n_pairs, scale):
    c_len = GLA_CHUNK
    half = LANES // 2
    ti = pl.program_id(1)

    @pl.when(ti == 0)
    def _():
        st_ref[...] = jnp.zeros_like(st_ref)

    lower = (_iota((c_len, c_len), 0) >= _iota((c_len, c_len), 1)).astype(BF16)
    lane = _iota((c_len, LANES), 1)
    first = lane < half
    r2 = _iota((2 * c_len, 2 * c_len), 0)
    c2 = _iota((2 * c_len, 2 * c_len), 1)
    sh = c_len.bit_length() - 1
    att_mask = jnp.logical_and((r2 >> sh) == (c2 >> sh), (c2 & (c_len - 1)) <= (r2 & (c_len - 1)))
    lane_sq = _iota((LANES, LANES), 1) < half
    ng = ng_ref[...]

    for c in range(tg // c_len):
        rows = slice(c * c_len, (c + 1) * c_len)
        la = la_ref[rows, :]
        b = _dot_split(la, lower, 3, left=True)
        b_last = b[c_len - 1:c_len, :]
        qd = q_ref[rows, :] * scale * jnp.exp(b)
        kk = k_ref[rows, :]
        kd = kk * jnp.exp(-b)
        ku = kk * jnp.exp(b_last - b)
        e_last = jnp.exp(b_last)
        for pr in range(n_pairs):
            ls = slice(pr * LANES, (pr + 1) * LANES)
            qd_p = qd[:, ls]
            qs = jnp.concatenate([jnp.where(first, qd_p, 0.0), jnp.where(first, 0.0, qd_p)],
                                 axis=0).astype(BF16)
            kd_p = kd[:, ls].astype(BF16)
            kd2 = jnp.concatenate([kd_p, kd_p], axis=0)
            att = jnp.where(att_mask, _dot_nt(qs, kd2), 0.0).astype(BF16)
            v_pair = v_ref[rows, 2 * pr * LANES:(2 * pr + 2) * LANES].astype(BF16)
            v_stack = jnp.concatenate([v_pair[:, :LANES], v_pair[:, LANES:]], axis=0)
            st = st_ref[pr]
            o = _dot(att, v_stack) + _dot_nt(qs, st.astype(BF16))
            upd = _dot_tn(v_pair, ku[:, ls].astype(BF16))
            st_ref[pr] = st * e_last[:, ls] + jnp.where(lane_sq, upd[:LANES], upd[LANES:])
            for s in range(2):
                hh = 2 * pr + s
                cs = slice(hh * LANES, (hh + 1) * LANES)
                oh = _rms(o[s * c_len:(s + 1) * c_len], ng)
                y_ref[rows, cs] = (oh * _silu(g_ref[rows, cs])).astype(y_ref.dtype)

    @pl.when(ti == pl.num_programs(1) - 1)
    def _():
        for pr in range(n_pairs):
            s_t = st_ref[pr].T
            s_ref[2 * pr] = s_t[:half]
            s_ref[2 * pr + 1] = s_t[half:]


def _gla_prompt(l_arr, p, la, ng, *, batch, n_heads, dk, dv, tg, cq, ck, cv, cg):
    m = p.shape[0]
    t = m // batch
    nt = t // tg
    wk = n_heads * dk
    wv = n_heads * dv
    n_pairs = n_heads // 2
    return pl.pallas_call(
        functools.partial(_gla_prompt_kernel, tg=tg, n_pairs=n_pairs, scale=dk ** -0.5),
        grid_spec=pltpu.PrefetchScalarGridSpec(
            num_scalar_prefetch=1,
            grid=(batch, nt),
            in_specs=[
                pl.BlockSpec((tg, wk), lambda b, i, l: (b * nt + i, cq)),
                pl.BlockSpec((tg, wk), lambda b, i, l: (b * nt + i, ck)),
                pl.BlockSpec((tg, wv), lambda b, i, l: (b * nt + i, cv)),
                pl.BlockSpec((tg, wv), lambda b, i, l: (b * nt + i, cg)),
                pl.BlockSpec((tg, wk), lambda b, i, l: (b * nt + i, 0)),
                pl.BlockSpec((None, 1, dv), lambda b, i, l: (l[0], 0, 0)),
            ],
            out_specs=[
                pl.BlockSpec((tg, wv), lambda b, i, l: (b * nt + i, 0)),
                pl.BlockSpec((None, n_heads, dk, dv), lambda b, i, l: (b, 0, 0, 0)),
            ],
            scratch_shapes=[pltpu.VMEM((n_pairs, LANES, LANES), F32)],
        ),
        out_shape=[jax.ShapeDtypeStruct((m, wv), BF16),
                   jax.ShapeDtypeStruct((batch, n_heads, dk, dv), F32)],
        compiler_params=pltpu.CompilerParams(dimension_semantics=("arbitrary", "arbitrary")),
        name="gla_prompt",
    )(l_arr, p, p, p, p, la, ng)


def _out_kernel(l_ref, x_ref, ya_ref, yb_ref, yc_ref, w0, w1, w2, w3, gain_ref, o_ref, *h_ref,
                final):
    wq = w0.shape[0]
    yb = yb_ref[...].astype(BF16)
    acc = x_ref[...] + _dot(ya_ref[...].astype(BF16), w0[...])
    acc = acc + _dot(yb[:, :wq], w1[...]) + _dot(yb[:, wq:], w2[...])
    acc = acc + _dot(yc_ref[...].astype(BF16), w3[...])
    if final:
        o_ref[...] = _rms(acc, gain_ref[...])
    else:
        o_ref[...] = acc
        h_ref[0][...] = _rms(acc, gain_ref[...]).astype(BF16)


def _out_proj(l_arr, x, ya, yb, yc, w_out, gain, *, tm, final):
    m, d = x.shape
    wq = ya.shape[1]
    assert yb.shape[1] == 2 * wq and yc.shape[1] == wq and w_out.shape[1] == 4 * wq
    wspec = lambda r: pl.BlockSpec((None, wq, d), lambda i, l, r=r: (l[0], r, 0))
    row_spec = pl.BlockSpec((tm, d), lambda i, l: (i, 0))
    out_specs = row_spec if final else [row_spec, row_spec]
    out_shape = jax.ShapeDtypeStruct((m, d), F32)
    if not final:
        out_shape = [out_shape, jax.ShapeDtypeStruct((m, d), BF16)]
    return pl.pallas_call(
        functools.partial(_out_kernel, final=final),
        grid_spec=pltpu.PrefetchScalarGridSpec(
            num_scalar_prefetch=1,
            grid=(m // tm,),
            in_specs=[
                pl.BlockSpec((tm, d), lambda i, l: (i, 0)),
                pl.BlockSpec((tm, wq), lambda i, l: (i, 0)),
                pl.BlockSpec((tm, 2 * wq), lambda i, l: (i, 0)),
                pl.BlockSpec((tm, wq), lambda i, l: (i, 0)),
                wspec(0), wspec(1), wspec(2), wspec(3),
                pl.BlockSpec((1, d), lambda i, l: (0, 0)),
            ],
            out_specs=out_specs,
        ),
        out_shape=out_shape,
        compiler_params=pltpu.CompilerParams(dimension_semantics=("arbitrary",)),
        name="out_proj",
    )(l_arr, x, ya, yb, yc, w_out, w_out, w_out, w_out, gain)


def _rev_excl_cumsum(x, later, parts):
    r, n = x.shape
    nb = n // LANES
    xs = jnp.concatenate([x[:, i * LANES:(i + 1) * LANES] for i in range(nb)], axis=0)
    cs = _dot_split(xs, later, parts)
    tot = jnp.sum(xs, axis=1, keepdims=True)
    run = jnp.zeros((r, 1), F32)
    blocks = [None] * nb
    for i in range(nb - 1, -1, -1):
        blocks[i] = cs[i * r:(i + 1) * r] + run
        run = run + tot[i * r:(i + 1) * r]
    return jnp.concatenate(blocks, axis=1), run


def _head_rows(row_vec, h):
    part = jnp.broadcast_to(row_vec[:, h * LANES:(h + 1) * LANES], (SUBLANES, LANES))
    return jnp.where(_iota((SUBLANES, LANES), 0) == h, part, 0.0)


def _head_block(buf, slot, h, tok, n_heads):
    return buf[slot, pl.ds(h, tok, stride=n_heads), :].astype(BF16)


def _paged_scores(q_row, kbuf, slot, tok, n_heads):
    z = None
    for h in range(n_heads):
        t = _dot_nt(_head_rows(q_row, h).astype(BF16), _head_block(kbuf, slot, h, tok, n_heads))
        z = t if z is None else z + t
    return z


def _own_rows(acc_ref, n_heads, denom=None):
    parts = []
    for h in range(n_heads):
        a = acc_ref[h] if denom is None else acc_ref[h] / denom
        parts.append(a[h:h + 1, :])
    return jnp.concatenate(parts, axis=1)


def _page_copies(pt_ref, l, bb, gg, slot, srcs, bufs, sem, *, n_groups, g_pages):
    base = (n_groups - 1 - gg) * g_pages
    out = []
    for r in range(g_pages):
        pg = pt_ref[bb, base + r]
        for si, (src, buf) in enumerate(zip(srcs, bufs)):
            rows = src.shape[2]
            out.append(pltpu.make_async_copy(
                src.at[l, pg], buf.at[slot, pl.ds(r * rows, rows)], sem.at[si, slot]))
    return out


def _paged_pipeline(pt_ref, l, srcs, bufs, sem, *, n_groups, g_pages):
    bb = pl.program_id(0)
    gg = pl.program_id(1)
    step = bb * n_groups + gg
    total = pl.num_programs(0) * n_groups
    slot = lax.rem(step, 2)
    mk = functools.partial(_page_copies, pt_ref, l, srcs=srcs, bufs=bufs, sem=sem,
                           n_groups=n_groups, g_pages=g_pages)

    @pl.when(step == 0)
    def _():
        for cp in mk(bb, gg, slot):
            cp.start()

    @pl.when(step + 1 < total)
    def _():
        wrap = gg + 1 == n_groups
        nb = jnp.where(wrap, bb + 1, bb)
        ng = jnp.where(wrap, 0, gg + 1)
        for cp in mk(nb, ng, 1 - slot):
            cp.start()

    for cp in mk(bb, gg, slot):
        cp.wait()
    return slot


def _sb_decode_kernel(pt_ref, l_ref, p_ref, kc_ref, vc_ref, o_ref, kbuf, vbuf, sem, acc_ref,
                      carry_ref, flag_ref, *, n_groups, g_pages, tok, n_heads, scale, cq, cg):
    bb = pl.program_id(0)
    gg = pl.program_id(1)
    w = n_heads * LANES
    step = bb * n_groups + gg
    slot = lax.rem(step, 2)
    mk = functools.partial(_page_copies, pt_ref, l_ref[0], srcs=(kc_ref, vc_ref),
                           bufs=(kbuf, vbuf), sem=sem, n_groups=n_groups, g_pages=g_pages)

    @pl.when(step == 0)
    def _():
        for cp in mk(bb, gg, slot):
            cp.start()
        flag_ref[1] = 1

    @pl.when(gg == 0)
    def _():
        acc_ref[...] = jnp.zeros_like(acc_ref)
        carry_ref[...] = jnp.zeros_like(carry_ref)
        flag_ref[0] = 1

    live = flag_ref[0] > 0
    requested = flag_ref[1] > 0
    last_group = gg + 1 == n_groups
    want_next = jnp.where(last_group, bb + 1 < pl.num_programs(0), live)

    @pl.when(want_next)
    def _():
        nb = jnp.where(last_group, bb + 1, bb)
        ng = jnp.where(last_group, 0, gg + 1)
        for cp in mk(nb, ng, 1 - slot):
            cp.start()

    @pl.when(requested)
    def _():
        for cp in mk(bb, gg, slot):
            cp.wait()

    @pl.when(jnp.logical_and(requested, live))
    def _():
        q_row = p_ref[:, cq * LANES:cq * LANES + w] * scale
        later = (_iota((LANES, LANES), 0) > _iota((LANES, LANES), 1)).astype(BF16)
        z = _paged_scores(q_row, kbuf, slot, tok, n_heads)
        l1m = -_softplus(z)
        cum, tot = _rev_excl_cumsum(l1m, later, 2)
        wgt = jnp.exp(z + l1m + cum + carry_ref[...]).astype(BF16)
        for h in range(n_heads):
            acc_ref[h] += _dot(wgt, _head_block(vbuf, slot, h, tok, n_heads))
        carry = carry_ref[...] + tot
        carry_ref[...] = carry
        heads = _iota(carry.shape, 0) < n_heads
        flag_ref[0] = (jnp.max(jnp.where(heads, carry, NEG_BIG)) > SB_DEAD).astype(jnp.int32)

    flag_ref[1] = want_next.astype(jnp.int32)

    @pl.when(gg == n_groups - 1)
    def _():
        gate = p_ref[:, cg * LANES:cg * LANES + w]
        o_ref[...] = _own_rows(acc_ref, n_heads) * _silu(gate)


def _sb_decode(page_table, l_arr, p, kc, vc, *, n_heads, g_pages, cq, cg):
    n_seq, n_pages = page_table.shape
    rows = kc.shape[2]
    w = n_heads * LANES
    n_groups = n_pages // g_pages
    tok = g_pages * rows // n_heads
    return pl.pallas_call(
        functools.partial(_sb_decode_kernel, n_groups=n_groups, g_pages=g_pages, tok=tok,
                          n_heads=n_heads, scale=LANES ** -0.5, cq=cq, cg=cg),
        grid_spec=pltpu.PrefetchScalarGridSpec(
            num_scalar_prefetch=2,
            grid=(n_seq, n_groups),
            in_specs=[
                pl.BlockSpec((None, 1, p.shape[2]), lambda b, g, pt, l: (b, 0, 0)),
                pl.BlockSpec(memory_space=pl.ANY),
                pl.BlockSpec(memory_space=pl.ANY),
            ],
            out_specs=pl.BlockSpec((None, 1, w), lambda b, g, pt, l: (b, 0, 0)),
            scratch_shapes=[
                pltpu.VMEM((2, g_pages * rows, LANES), F32),
                pltpu.VMEM((2, g_pages * rows, LANES), F32),
                pltpu.SemaphoreType.DMA((2, 2)),
                pltpu.VMEM((n_heads, SUBLANES, LANES), F32),
                pltpu.VMEM((SUBLANES, 1), F32),
                pltpu.SMEM((2,), jnp.int32),
            ],
        ),
        out_shape=jax.ShapeDtypeStruct((n_seq, 1, w), F32),
        compiler_params=pltpu.CompilerParams(dimension_semantics=("arbitrary", "arbitrary")),
        name="sb_decode",
    )(page_table, l_arr, p, kc, vc)


def _fox_decode_kernel(pt_ref, l_ref, p_ref, lfn_ref, kc_ref, vc_ref, lc_ref, o_ref, kbuf, vbuf,
                       lbuf, sem, acc_ref, m_ref, l_sum_ref, carry_ref, *, n_groups, g_pages, tok,
                       n_heads, scale, cq, ck, cv, cg):
    gg = pl.program_id(1)
    w = n_heads * LANES
    slot = _paged_pipeline(pt_ref, l_ref[0], (kc_ref, vc_ref, lc_ref), (kbuf, vbuf, lbuf), sem,
                           n_groups=n_groups, g_pages=g_pages)
    q_row = p_ref[:, cq * LANES:cq * LANES + w] * scale

    @pl.when(gg == 0)
    def _():
        k_new = p_ref[:, ck * LANES:ck * LANES + w]
        v_new = p_ref[:, cv * LANES:cv * LANES + w]
        z_new = jnp.zeros((SUBLANES, 1), F32)
        for h in range(n_heads):
            cs = slice(h * LANES, (h + 1) * LANES)
            z_new = z_new + jnp.sum(_head_rows(q_row, h) * k_new[:, cs], axis=1, keepdims=True)
            acc_ref[h] = jnp.broadcast_to(v_new[:, cs], (SUBLANES, LANES))
        m_ref[...] = z_new
        l_sum_ref[...] = jnp.ones_like(l_sum_ref)
        pick = _iota((SUBLANES, LANES), 1) == _iota((SUBLANES, LANES), 0) + SIDE_FC
        lf_row = jnp.broadcast_to(lfn_ref[...], (SUBLANES, LANES))
        carry_ref[...] = jnp.sum(jnp.where(pick, lf_row, 0.0), axis=1, keepdims=True)

    later = (_iota((LANES, LANES), 0) > _iota((LANES, LANES), 1)).astype(BF16)
    lf = jnp.concatenate([lbuf[slot, pl.ds(r * SUBLANES, SUBLANES), :] for r in range(g_pages)],
                         axis=1)
    bias, tot = _rev_excl_cumsum(lf, later, 3)
    z = _paged_scores(q_row, kbuf, slot, tok, n_heads) + (bias + carry_ref[...])
    m_old = m_ref[...]
    m_new = jnp.maximum(m_old, jnp.max(z, axis=1, keepdims=True))
    alpha = jnp.exp(m_old - m_new)
    pr = jnp.exp(z - m_new)
    l_sum_ref[...] = alpha * l_sum_ref[...] + jnp.sum(pr, axis=1, keepdims=True)
    pr = pr.astype(BF16)
    for h in range(n_heads):
        acc_ref[h] = alpha * acc_ref[h] + _dot(pr, _head_block(vbuf, slot, h, tok, n_heads))
    m_ref[...] = m_new
    carry_ref[...] += tot

    @pl.when(gg == n_groups - 1)
    def _():
        gate = p_ref[:, cg * LANES:cg * LANES + w]
        o_ref[...] = _own_rows(acc_ref, n_heads, l_sum_ref[...]) * _silu(gate)


def _fox_decode(page_table, l_arr, p, lf_new, kc, vc, lc, *, n_heads, g_pages, cq, ck, cv, cg):
    n_seq, n_pages = page_table.shape
    rows = kc.shape[2]
    w = n_heads * LANES
    n_groups = n_pages // g_pages
    tok = g_pages * rows // n_heads
    return pl.pallas_call(
        functools.partial(_fox_decode_kernel, n_groups=n_groups, g_pages=g_pages, tok=tok,
                          n_heads=n_heads, scale=LANES ** -0.5, cq=cq, ck=ck, cv=cv, cg=cg),
        grid_spec=pltpu.PrefetchScalarGridSpec(
            num_scalar_prefetch=2,
            grid=(n_seq, n_groups),
            in_specs=[
                pl.BlockSpec((None, 1, p.shape[2]), lambda b, g, pt, l: (b, 0, 0)),
                pl.BlockSpec((None, 1, lf_new.shape[2]), lambda b, g, pt, l: (b, 0, 0)),
                pl.BlockSpec(memory_space=pl.ANY),
                pl.BlockSpec(memory_space=pl.ANY),
                pl.BlockSpec(memory_space=pl.ANY),
            ],
            out_specs=pl.BlockSpec((None, 1, w), lambda b, g, pt, l: (b, 0, 0)),
            scratch_shapes=[
                pltpu.VMEM((2, g_pages * rows, LANES), F32),
                pltpu.VMEM((2, g_pages * rows, LANES), F32),
                pltpu.VMEM((2, g_pages * SUBLANES, LANES), F32),
                pltpu.SemaphoreType.DMA((3, 2)),
                pltpu.VMEM((n_heads, SUBLANES, LANES), F32),
                pltpu.VMEM((SUBLANES, 1), F32),
                pltpu.VMEM((SUBLANES, 1), F32),
                pltpu.VMEM((SUBLANES, 1), F32),
            ],
        ),
        out_shape=jax.ShapeDtypeStruct((n_seq, 1, w), F32),
        compiler_params=pltpu.CompilerParams(dimension_semantics=("arbitrary", "arbitrary")),
        name="fox_decode",
    )(page_table, l_arr, p, lf_new, kc, vc, lc)


def _gla_decode_kernel(l_ref, p_ref, la_ref, s0_ref, ng_ref, y_ref, s_ref, *, n_heads, dk, dv,
                       scale, cq, ck, cv, cg):
    bb = pl.program_id(0)
    wk = n_heads * dk

    q_row = p_ref[:,cq * LANES:cq * LANES + wk]
    k_row = p_ref[:,ck * LANES:ck * LANES + wk]
    e_row = jnp.exp(la_ref[...])
    lane = _iota((dk, wk), 1)
    sub = _iota((dk, wk), 0)
    ng = ng_ref[...]

    def column(row_vec, hh):
        pick = lane == sub + hh * dk
        return jnp.sum(jnp.where(pick, jnp.broadcast_to(row_vec, (dk, wk)), 0.0),
                       axis=1, keepdims=True)

    for hh in range(n_heads):
        cs = slice(hh * dv, (hh + 1) * dv)
        v_row = p_ref[:,cv * LANES + hh * dv:cv * LANES + (hh + 1) * dv]
        s_new = column(e_row, hh) * s0_ref[hh] + column(k_row, hh) * v_row
        s_ref[hh] = s_new
        o = jnp.sum((column(q_row, hh) * scale) * s_new, axis=0, keepdims=True)
        gate = p_ref[:,cg * LANES + hh * dv:cg * LANES + (hh + 1) * dv]
        y_ref[:, cs] = _rms(o, ng) * _silu(gate)


def _gla_decode(l_arr, p, la, state, ng, *, cq, ck, cv, cg):
    _, n_seq, n_heads, dk, dv = state.shape
    return pl.pallas_call(
        functools.partial(_gla_decode_kernel, n_heads=n_heads, dk=dk, dv=dv, scale=dk ** -0.5,
                          cq=cq, ck=ck, cv=cv, cg=cg),
        grid_spec=pltpu.PrefetchScalarGridSpec(
            num_scalar_prefetch=1,
            grid=(n_seq,),
            in_specs=[
                pl.BlockSpec((None, 1, p.shape[2]), lambda b, l: (b, 0, 0)),
                pl.BlockSpec((None, 1, la.shape[2]), lambda b, l: (b, 0, 0)),
                pl.BlockSpec((None, None, n_heads, dk, dv), lambda b, l: (l[0], b, 0, 0, 0)),
                pl.BlockSpec((None, 1, dv), lambda b, l: (l[0], 0, 0)),
            ],
            out_specs=[
                pl.BlockSpec((None, 1, n_heads * dv), lambda b, l: (b, 0, 0)),
                pl.BlockSpec((None, n_heads, dk, dv), lambda b, l: (b, 0, 0, 0)),
            ],
        ),
        out_shape=[jax.ShapeDtypeStruct((n_seq, 1, n_heads * dv), F32),
                   jax.ShapeDtypeStruct((n_seq, n_heads, dk, dv), F32)],
        compiler_params=pltpu.CompilerParams(dimension_semantics=("arbitrary",)),
        name="gla_decode",
    )(l_arr, p, la, state, ng)


DEC_ROWS = 16
PROJ_TM = 1024
SIDE_TA = 512
ATT_TB = 256
GLA_TG = 256
OUT_TM = 512
PAGES_PER_STEP = 16
SB_PAGES_PER_STEP = 8


def kernel(x_prompt, x_sample, cache_sb_k, cache_sb_v, cache_fox_k, cache_fox_v, cache_fox_logf,
           state_gla, page_table, norm_g, w_in, w_gate_b, b_gate, b_forget, q_norm_g, k_norm_g,
           gla_norm_g, w_out, norm_f):
    batch, seq, d = x_prompt.shape
    n_seq = x_sample.shape[0]
    depth, n_pool, page, h_sb, hd = cache_sb_k.shape
    h_fox = cache_fox_k.shape[3]
    _, _, h_gla, dk, dv = state_gla.shape
    rank = w_gate_b.shape[1]
    assert hd == LANES and dv == LANES and 2 * dk == LANES and x_sample.shape[1] == 1
    w_sb, w_fox, w_gk, w_gv = h_sb * hd, h_fox * hd, h_gla * dk, h_gla * dv
    assert w_sb == w_fox == w_gk and w_gv == 2 * w_sb

    sizes = (w_sb,) * 4 + (w_gk, w_gk, w_gv, w_gv, rank) + (w_fox,) * 4 + (h_fox,)
    offs = [0]
    for s in sizes:
        offs.append(offs[-1] + s)
    assert offs[-1] == w_in.shape[2]
    o_ab, o_qc, o_fc = offs[8], offs[9], offs[13]
    n_main = o_ab + (o_fc - o_qc)
    c_qa, c_ka, c_va, c_ga = (offs[i] // LANES for i in range(4))
    c_qb, c_kb, c_vb, c_gb = (offs[i] // LANES for i in range(4, 8))
    c_qc, c_kc, c_vc, c_gc = ((offs[i] - rank) // LANES for i in range(9, 13))

    assert SIDE_AB + rank <= SIDE_FC and SIDE_FC + h_fox <= LANES
    w_t = jnp.swapaxes(w_in, 1, 2).astype(BF16)
    w_b = w_t[:, o_qc:o_fc]
    w_side = jnp.zeros((depth, LANES, d), BF16)
    w_side = w_side.at[:, SIDE_AB:SIDE_AB + rank].set(w_t[:, o_ab:o_qc])
    w_side = w_side.at[:, SIDE_FC:SIDE_FC + h_fox].set(w_t[:, o_fc:])
    side = _side
    wgb = jnp.zeros((depth, LANES, w_gk), F32).at[:, SIDE_AB:SIDE_AB + rank].set(w_gate_b)
    wgb = wgb.astype(BF16)
    bg = b_gate.reshape(depth, 1, w_gk)
    bfg = jnp.zeros((depth, 1, LANES), F32).at[:, 0, SIDE_FC:SIDE_FC + h_fox].set(b_forget)
    w_o = w_out.astype(BF16)
    ng3 = norm_g.reshape(depth, 1, d)
    qg3 = q_norm_g.reshape(depth, 1, hd)
    kg3 = k_norm_g.reshape(depth, 1, hd)
    gg3 = gla_norm_g.reshape(depth, 1, dv)
    nf2 = norm_f.reshape(1, d)

    kc_sb = cache_sb_k.reshape(depth, n_pool, page * h_sb, hd)
    vc_sb = cache_sb_v.reshape(depth, n_pool, page * h_sb, hd)
    kc_fx = cache_fox_k.reshape(depth, n_pool, page * h_fox, hd)
    vc_fx = cache_fox_v.reshape(depth, n_pool, page * h_fox, hd)
    lc_fx = jnp.pad(jnp.swapaxes(cache_fox_logf, 2, 3), ((0, 0), (0, 0), (0, SUBLANES - h_fox), (0, 0)))

    m_p = batch * seq
    hp = x_prompt.reshape(m_p, d)
    hs = jnp.pad(x_sample.reshape(n_seq, d), ((0, DEC_ROWS - n_seq), (0, 0)))

    tn = 2 * w_fox
    assert o_ab % tn == 0 and (o_fc - o_qc) % tn == 0
    proj = functools.partial(_proj, tn=tn, ja=o_ab // tn)

    l0 = jnp.zeros((1,), jnp.int32)
    hp_n = _norm(l0, hp, ng3, tm=min(OUT_TM, m_p))
    hs_n = _norm(l0, hs, ng3, tm=DEC_ROWS)
    outs = [[] for _ in range(12)]
    for layer in range(depth):
        l_arr = jnp.full((1,), layer, jnp.int32)
        final = layer == depth - 1
        gain = nf2 if final else ng3[layer + 1]

        p = proj(l_arr, hp_n, w_t, w_b, qg3, kg3, tm=min(PROJ_TM, m_p))
        la, lf, fq, ft = side(l_arr, hp_n, w_side, wgb, bg, bfg, batch=batch,
                              ta=min(SIDE_TA, seq), n_heads=h_fox, with_cumsum=True)
        tb = min(ATT_TB, seq)
        ya, ka_o, va_o = _sb_prompt(p, batch=batch, n_heads=h_sb, tb=tb,
                                    cq=c_qa, ck=c_ka, cv=c_va, cg=c_ga)
        yb, s_fin = _gla_prompt(l_arr, p, la, gg3, batch=batch, n_heads=h_gla, dk=dk, dv=dv,
                                tg=min(GLA_TG, seq), cq=c_qb * LANES // w_gk, ck=c_kb * LANES // w_gk,
                                cv=c_vb * LANES // w_gv, cg=c_gb * LANES // w_gv)
        yc, kc_o, vc_o = _fox_prompt(p, fq, ft, batch=batch, n_heads=h_fox, tb=tb,
                         cq=c_qc, ck=c_kc, cv=c_vc, cg=c_gc)
        hp = _out_proj(l_arr, hp, ya, yb, yc, w_o, gain, tm=min(OUT_TM, m_p), final=final)
        if not final:
            hp, hp_n = hp
        outs[0].append(ka_o.reshape(batch, seq, h_sb, hd))
        outs[1].append(va_o.reshape(batch, seq, h_sb, hd))
        outs[2].append(kc_o.reshape(batch, seq, h_fox, hd))
        outs[3].append(vc_o.reshape(batch, seq, h_fox, hd))
        outs[4].append(lf[:, SIDE_FC:SIDE_FC + h_fox].reshape(batch, seq, h_fox))
        outs[5].append(s_fin)

        ps = proj(l_arr, hs_n, w_t, w_b, qg3, kg3, tm=DEC_ROWS)
        las, lfs = side(l_arr, hs_n, w_side, wgb, bg, bfg, batch=1, ta=DEC_ROWS,
                        n_heads=h_fox, with_cumsum=False)
        ps3 = ps.reshape(DEC_ROWS, 1, -1)
        g_pages = min(PAGES_PER_STEP, page_table.shape[1])
        yas = _sb_decode(page_table, l_arr, ps3, kc_sb, vc_sb, n_heads=h_sb,
                         g_pages=min(SB_PAGES_PER_STEP, page_table.shape[1]), cq=c_qa, cg=c_ga)
        ybs, s_new = _gla_decode(l_arr, ps3, las.reshape(DEC_ROWS, 1, -1), state_gla, gg3,
                                 cq=c_qb, ck=c_kb, cv=c_vb, cg=c_gb)
        ycs = _fox_decode(page_table, l_arr, ps3, lfs.reshape(DEC_ROWS, 1, -1), kc_fx, vc_fx, lc_fx,
                          n_heads=h_fox, g_pages=g_pages, cq=c_qc, ck=c_kc, cv=c_vc, cg=c_gc)
        pad_rows = lambda y: jnp.pad(y.reshape(n_seq, -1), ((0, DEC_ROWS - n_seq), (0, 0)))
        hs = _out_proj(l_arr, hs, pad_rows(yas), pad_rows(ybs), pad_rows(ycs), w_o, gain,
                       tm=DEC_ROWS, final=final)
        if not final:
            hs, hs_n = hs
        cols = lambda c, w: ps[:n_seq, c * LANES:c * LANES + w]
        outs[6].append(cols(c_ka, w_sb).reshape(n_seq, 1, h_sb, hd))
        outs[7].append(cols(c_va, w_sb).reshape(n_seq, 1, h_sb, hd))
        outs[8].append(cols(c_kc, w_fox).reshape(n_seq, 1, h_fox, hd))
        outs[9].append(cols(c_vc, w_fox).reshape(n_seq, 1, h_fox, hd))
        outs[10].append(lfs[:n_seq, SIDE_FC:SIDE_FC + h_fox].reshape(n_seq, 1, h_fox))
        outs[11].append(s_new)

    y_prompt = hp.reshape(batch, seq, d)
    y_sample = hs[:n_seq].reshape(n_seq, 1, d)
    return (y_prompt, y_sample) + tuple(jnp.stack(o) for o in outs)
```

```python
import functools

import jax
import jax.numpy as jnp
from jax import lax
from jax.experimental import pallas as pl
from jax.experimental.pallas import tpu as pltpu

F32 = jnp.float32
BF16 = jnp.bfloat16

LANES = 128
SUBLANES = 8
RMS_EPS = 1e-6
GATE_TAU = 16.0
GLA_CHUNK = 64
NEG_BIG = -1e30
SB_DEAD = -110.0

SIDE_FC = 16
SIDE_AB = 0


def _dot(a, b):
    return jnp.dot(a, b, preferred_element_type=F32)


def _dot_nt(a, b):
    return lax.dot_general(a, b, (((1,), (1,)), ((), ())), preferred_element_type=F32)


def _dot_tn(a, b):
    return lax.dot_general(a, b, (((0,), (0,)), ((), ())), preferred_element_type=F32)


def _softplus(z):
    return jnp.maximum(z, 0.0) + jnp.log(1.0 + jnp.exp(-jnp.abs(z)))


def _log_sigmoid(z):
    return -_softplus(-z)


def _silu(g):
    return g / (1.0 + jnp.exp(-g))


def _rms(x, g):
    ms = jnp.mean(x * x, axis=-1, keepdims=True)
    return x * lax.rsqrt(ms + RMS_EPS) * g


def _split_bf16(x, parts):
    out = []
    r = x
    for _ in range(parts):
        p = r.astype(BF16)
        out.append(p)
        r = r - p.astype(F32)
    return out


def _dot_split(x, m, parts, left=False):
    acc = None
    for p in _split_bf16(x, parts):
        t = _dot(m, p) if left else _dot(p, m)
        acc = t if acc is None else acc + t
    return acc


def _iota(shape, dim):
    return lax.broadcasted_iota(jnp.int32, shape, dim)


def _norm_kernel(l_ref, x_ref, g_ref, h_ref):
    h_ref[...] = _rms(x_ref[...], g_ref[...]).astype(BF16)


def _norm(l_arr, x, norm_g, *, tm):
    m, d = x.shape
    return pl.pallas_call(
        _norm_kernel,
        grid_spec=pltpu.PrefetchScalarGridSpec(
            num_scalar_prefetch=1,
            grid=(m // tm,),
            in_specs=[
                pl.BlockSpec((tm, d), lambda i, l: (i, 0)),
                pl.BlockSpec((None, 1, d), lambda i, l: (l[0], 0, 0)),
            ],
            out_specs=pl.BlockSpec((tm, d), lambda i, l: (i, 0)),
        ),
        out_shape=jax.ShapeDtypeStruct((m, d), BF16),
        compiler_params=pltpu.CompilerParams(dimension_semantics=("arbitrary",)),
        name="norm",
    )(l_arr, x, norm_g)


def _proj_kernel(l_ref, h_ref, wa_ref, wb_ref, qg_ref, kg_ref, o_ref, *, ja, jn):
    j = pl.program_id(0)

    @pl.when(j < ja)
    def _():
        o_ref[...] = _dot_nt(h_ref[...], wa_ref[...])

    @pl.when(jnp.logical_and(j >= ja, j != jn))
    def _():
        o_ref[...] = _dot_nt(h_ref[...], wb_ref[...])

    @pl.when(j == jn)
    def _():
        acc = _dot_nt(h_ref[...], wb_ref[...])
        n_h = acc.shape[1] // LANES
        for hh in range(n_h):
            cs = slice(hh * LANES, (hh + 1) * LANES)
            g = qg_ref[...] if hh < n_h // 2 else kg_ref[...]
            o_ref[:, cs] = _rms(acc[:, cs], g)


def _proj(l_arr, h, w_t, w_b, qg, kg, *, tm, tn, ja):
    m, d = h.shape
    jb = w_b.shape[1] // tn
    jn = ja
    return pl.pallas_call(
        functools.partial(_proj_kernel, ja=ja, jn=jn),
        grid_spec=pltpu.PrefetchScalarGridSpec(
            num_scalar_prefetch=1,
            grid=(ja + jb, m // tm),
            in_specs=[
                pl.BlockSpec((tm, d), lambda j, i, l: (i, 0)),
                pl.BlockSpec((None, tn, d), lambda j, i, l: (l[0], jnp.minimum(j, ja - 1), 0)),
                pl.BlockSpec((None, tn, d), lambda j, i, l: (l[0], jnp.maximum(j - ja, 0), 0)),
                pl.BlockSpec((None, 1, LANES), lambda j, i, l: (l[0], 0, 0)),
                pl.BlockSpec((None, 1, LANES), lambda j, i, l: (l[0], 0, 0)),
            ],
            out_specs=pl.BlockSpec((tm, tn), lambda j, i, l: (i, j)),
        ),
        out_shape=jax.ShapeDtypeStruct((m, (ja + jb) * tn), F32),
        compiler_params=pltpu.CompilerParams(dimension_semantics=("arbitrary", "arbitrary")),
        name="proj",
    )(l_arr, h, w_t, w_b, qg, kg)


def _side_kernel(l_ref, h_ref, ws_ref, wgb_ref, bg_ref, bf_ref, la_ref, lf_ref, *rest,
                 n_heads, with_cumsum):
    s = _dot_nt(h_ref[...], ws_ref[...])
    la_ref[...] = _log_sigmoid(_dot(s.astype(BF16), wgb_ref[...]) + bg_ref[...]) / GATE_TAU
    lf = _log_sigmoid(s + bf_ref[...])
    lf_ref[...] = lf
    if with_cumsum:
        fq_ref, ft_ref, carry_ref = rest
        ta = lf.shape[0]

        @pl.when(pl.program_id(1) == 0)
        def _():
            carry_ref[...] = jnp.zeros_like(carry_ref)

        lower = (_iota((ta, ta), 0) >= _iota((ta, ta), 1)).astype(BF16)
        f = _dot_split(lf, lower, 3, left=True) + carry_ref[...]
        carry_ref[...] = f[ta - 1:ta, :]
        ft = f.T
        for hh in range(n_heads):
            c = SIDE_FC + hh
            fq_ref[:, hh * LANES:(hh + 1) * LANES] = jnp.broadcast_to(f[:, c:c + 1], (ta, LANES))
            ft_ref[hh] = ft[c:c + 1, :]


def _side(l_arr, h, w_side, wgb, bg, bfg, *, batch, ta, n_heads, with_cumsum):
    m, d = h.shape
    t = m // batch
    nt = t // ta
    n_la = wgb.shape[2]
    in_specs = [
        pl.BlockSpec((ta, d), lambda b, i, l: (b * nt + i, 0)),
        pl.BlockSpec((None, LANES, d), lambda b, i, l: (l[0], 0, 0)),
        pl.BlockSpec((None, LANES, n_la), lambda b, i, l: (l[0], 0, 0)),
        pl.BlockSpec((None, 1, n_la), lambda b, i, l: (l[0], 0, 0)),
        pl.BlockSpec((None, 1, LANES), lambda b, i, l: (l[0], 0, 0)),
    ]
    out_specs = [
        pl.BlockSpec((ta, n_la), lambda b, i, l: (b * nt + i, 0)),
        pl.BlockSpec((ta, LANES), lambda b, i, l: (b * nt + i, 0)),
    ]
    out_shape = [jax.ShapeDtypeStruct((m, n_la), F32), jax.ShapeDtypeStruct((m, LANES), F32)]
    scratch = []
    if with_cumsum:
        out_specs += [
            pl.BlockSpec((ta, n_heads * LANES), lambda b, i, l: (b * nt + i, 0)),
            pl.BlockSpec((None, n_heads, 1, ta), lambda b, i, l: (b, 0, 0, i)),
        ]
        out_shape += [jax.ShapeDtypeStruct((m, n_heads * LANES), F32),
                      jax.ShapeDtypeStruct((batch, n_heads, 1, t), F32)]
        scratch = [pltpu.VMEM((1, LANES), F32)]
    return pl.pallas_call(
        functools.partial(_side_kernel, n_heads=n_heads, with_cumsum=with_cumsum),
        grid_spec=pltpu.PrefetchScalarGridSpec(
            num_scalar_prefetch=1, grid=(batch, nt), in_specs=in_specs, out_specs=out_specs,
            scratch_shapes=scratch),
        out_shape=out_shape,
        compiler_params=pltpu.CompilerParams(dimension_semantics=("arbitrary", "arbitrary")),
        name="side",
    )(l_arr, h, w_side, wgb, bg, bfg)


def _interleave_rows(dst_ref, src_refs, t, chunk):
    n = len(src_refs)

    def body(c, carry):
        off = pl.multiple_of(c * chunk, chunk)
        for h, src in enumerate(src_refs):
            dst_ref[pl.ds(off * n + h, chunk, stride=n), :] = src[pl.ds(off, chunk), :]
        return carry

    lax.fori_loop(0, t // chunk, body, 0)


def _attn_specs(p, k_all, *, batch, n_heads, tb, cq, ck, cv, cg):
    m = p.shape[0]
    t = m // batch
    nq = t // tb
    w = n_heads * LANES
    in_specs = [
        pl.BlockSpec((tb, w), lambda b, i, l: (b * nq + i, cq // n_heads)),
        pl.BlockSpec((tb, w), lambda b, i, l: (b * nq + i, cg // n_heads)),
    ]
    in_specs += [pl.BlockSpec((t, LANES), lambda b, i, l, c=ck + h: (b, c)) for h in range(n_heads)]
    in_specs += [pl.BlockSpec((t, LANES), lambda b, i, l, c=cv + h: (b, c)) for h in range(n_heads)]
    slab = pl.BlockSpec((None, None, t * n_heads, LANES), lambda b, i, l: (l[0], b, 0, 0))
    out_specs = [pl.BlockSpec((tb, w), lambda b, i, l: (b * nq + i, 0)), slab, slab]
    out_shape = [
        jax.ShapeDtypeStruct((m, w), BF16),
        jax.ShapeDtypeStruct(k_all.shape, F32),
        jax.ShapeDtypeStruct(k_all.shape, F32),
    ]
    return (batch, nq), in_specs, out_specs, out_shape


def _sb_prompt_kernel(lay_ref, q_ref, g_ref, *refs, tb, n_heads, scale):
    k_refs = refs[:n_heads]
    v_refs = refs[n_heads:2 * n_heads]
    _, _, o_ref, ko_ref, vo_ref, qs_ref, acc_ref, carry_ref = refs[2 * n_heads:]
    i = pl.program_id(1)
    t = k_refs[0].shape[0]

    @pl.when(i == 0)
    def _():
        _interleave_rows(ko_ref, k_refs, t, tb)
        _interleave_rows(vo_ref, v_refs, t, tb)

    row = _iota((2 * LANES, 2 * LANES), 0) & (LANES - 1)
    col = _iota((2 * LANES, 2 * LANES), 1)
    later2 = jnp.logical_or(row > col, col >= LANES).astype(BF16)
    valid = _iota((tb, tb), 1) < _iota((tb, tb), 0)
    for h in range(n_heads):
        qs_ref[h] = (q_ref[:, h * LANES:(h + 1) * LANES] * scale).astype(BF16)
    acc_ref[...] = jnp.zeros_like(acc_ref)
    carry_ref[...] = jnp.zeros_like(carry_ref)

    def block(h, j, diagonal):
        off = pl.multiple_of(j * tb, tb)
        k = k_refs[h][pl.ds(off, tb), :].astype(BF16)
        v = v_refs[h][pl.ds(off, tb), :].astype(BF16)
        z = _dot_nt(qs_ref[h], k)
        l1m = -_softplus(z)
        if diagonal:
            l1m = jnp.where(valid, l1m, 0.0)
        run = carry_ref[h]
        tiles = [None] * (tb // LANES)
        for c in range(tb // LANES - 1, -1, -1):
            ls = slice(c * LANES, (c + 1) * LANES)
            cum = _dot(jnp.concatenate(_split_bf16(l1m[:, ls], 2), axis=1), later2)
            tiles[c] = jnp.exp(z[:, ls] + l1m[:, ls] + (cum[:, :LANES] + run))
            run = run + cum[:, LANES:]
        w = jnp.concatenate(tiles, axis=1)
        if diagonal:
            w = jnp.where(valid, w, 0.0)
        acc_ref[h] += _dot(w.astype(BF16), v)
        carry_ref[h] = run

    def any_live():
        top = carry_ref[0]
        for h in range(1, n_heads):
            top = jnp.maximum(top, carry_ref[h])
        return (jnp.max(top) > SB_DEAD).astype(jnp.int32)

    for h in range(n_heads):
        block(h, i, True)

    def body(c):
        jj, _ = c
        for h in range(n_heads):
            block(h, i - 1 - jj, False)
        return jj + 1, any_live()

    lax.while_loop(lambda c: jnp.logical_and(c[0] < i, c[1] > 0), body, (0, any_live()))
    for h in range(n_heads):
        cs = slice(h * LANES, (h + 1) * LANES)
        o_ref[:, cs] = (acc_ref[h] * _silu(g_ref[:, cs])).astype(o_ref.dtype)


def _sb_prompt(l_arr, p, k_all, v_all, *, batch, n_heads, tb, cq, ck, cv, cg):
    grid, in_specs, out_specs, out_shape = _attn_specs(
        p, k_all, batch=batch, n_heads=n_heads, tb=tb, cq=cq, ck=ck, cv=cv, cg=cg)
    n_in = 1 + len(in_specs)
    in_specs += [pl.BlockSpec(memory_space=pl.ANY)] * 2
    return pl.pallas_call(
        functools.partial(_sb_prompt_kernel, tb=tb, n_heads=n_heads, scale=LANES ** -0.5),
        grid_spec=pltpu.PrefetchScalarGridSpec(
            num_scalar_prefetch=1,
            grid=grid,
            in_specs=in_specs,
            out_specs=out_specs,
            scratch_shapes=[
                pltpu.VMEM((n_heads, tb, LANES), BF16),
                pltpu.VMEM((n_heads, tb, LANES), F32),
                pltpu.VMEM((n_heads, tb, LANES), F32),
            ],
        ),
        out_shape=out_shape,
        input_output_aliases={n_in: 1, n_in + 1: 2},
        compiler_params=pltpu.CompilerParams(dimension_semantics=("arbitrary", "arbitrary")),
        name="sb_prompt",
    )(l_arr, *([p] * (2 + 2 * n_heads)), k_all, v_all)


def _fox_prompt_kernel(lay_ref, q_ref, g_ref, *refs, tb, n_heads, scale):
    k_refs = refs[:n_heads]
    v_refs = refs[n_heads:2 * n_heads]
    fq_ref, ft_ref, _, _, o_ref, ko_ref, vo_ref, qs_ref, acc_ref, m_ref, l_ref = refs[2 * n_heads:]
    i = pl.program_id(1)
    t = k_refs[0].shape[0]

    @pl.when(i == 0)
    def _():
        _interleave_rows(ko_ref, k_refs, t, tb)
        _interleave_rows(vo_ref, v_refs, t, tb)

    valid = _iota((tb, tb), 1) <= _iota((tb, tb), 0)
    for h in range(n_heads):
        qs_ref[h] = (q_ref[:, h * LANES:(h + 1) * LANES] * scale).astype(BF16)
    acc_ref[...] = jnp.zeros_like(acc_ref)
    l_ref[...] = jnp.zeros_like(l_ref)
    m_ref[...] = jnp.full_like(m_ref, NEG_BIG)

    def block(h, j, kw, diagonal):
        off = pl.multiple_of(j * tb, tb)
        wide = lambda a: jnp.concatenate([a] * (kw // LANES), axis=1)
        k = k_refs[h][pl.ds(off, kw), :].astype(BF16)
        v1 = jnp.concatenate([v_refs[h][pl.ds(off, kw), :].astype(BF16),
                              jnp.ones((kw, LANES), BF16)], axis=1)
        fq = fq_ref[:, h * LANES:(h + 1) * LANES]
        fk = ft_ref[h, :, pl.ds(off, kw)]
        z = _dot_nt(qs_ref[h], k) + (wide(fq) - fk)
        if diagonal:
            z = jnp.where(valid, z, NEG_BIG)
        m_old = m_ref[h]
        m_new = jnp.maximum(m_old, jnp.max(z, axis=1, keepdims=True))
        alpha = jnp.exp(m_old - m_new)
        pv = _dot(jnp.exp(z - wide(m_new)).astype(BF16), v1)
        acc_ref[h] = alpha * acc_ref[h] + pv[:, :LANES]
        l_ref[h] = alpha * l_ref[h] + pv[:, LANES:]
        m_ref[h] = m_new

    for h in range(n_heads):
        block(h, i, tb, True)

    def body(jj, c):
        for h in range(n_heads):
            block(h, 2 * jj, 2 * tb, False)
        return c

    lax.fori_loop(0, i // 2, body, 0)

    @pl.when(i % 2 == 1)
    def _():
        for h in range(n_heads):
            block(h, i - 1, tb, False)

    for h in range(n_heads):
        cs = slice(h * LANES, (h + 1) * LANES)
        o_ref[:, cs] = ((acc_ref[h] / l_ref[h]) * _silu(g_ref[:, cs])).astype(o_ref.dtype)


def _fox_prompt(l_arr, p, fq, ft, k_all, v_all, *, batch, n_heads, tb, cq, ck, cv, cg):
    grid, in_specs, out_specs, out_shape = _attn_specs(
        p, k_all, batch=batch, n_heads=n_heads, tb=tb, cq=cq, ck=ck, cv=cv, cg=cg)
    m = p.shape[0]
    t = m // batch
    nq = t // tb
    in_specs += [
        pl.BlockSpec((tb, n_heads * LANES), lambda b, i, l: (b * nq + i, 0)),
        pl.BlockSpec((None, n_heads, 1, t), lambda b, i, l: (b, 0, 0, 0)),
    ]
    n_in = 1 + len(in_specs)
    in_specs += [pl.BlockSpec(memory_space=pl.ANY)] * 2
    return pl.pallas_call(
        functools.partial(_fox_prompt_kernel, tb=tb, n_heads=n_heads, scale=LANES ** -0.5),
        grid_spec=pltpu.PrefetchScalarGridSpec(
            num_scalar_prefetch=1,
            grid=grid,
            in_specs=in_specs,
            out_specs=out_specs,
            scratch_shapes=[
                pltpu.VMEM((n_heads, tb, LANES), BF16),
                pltpu.VMEM((n_heads, tb, LANES), F32),
                pltpu.VMEM((n_heads, tb, LANES), F32),
                pltpu.VMEM((n_heads, tb, LANES), F32),
            ],
        ),
        out_shape=out_shape,
        input_output_aliases={n_in: 1, n_in + 1: 2},
        compiler_params=pltpu.CompilerParams(dimension_semantics=("arbitrary", "arbitrary")),
        name="fox_prompt",
    )(l_arr, *([p] * (2 + 2 * n_heads)), fq, ft, k_all, v_all)


def _gla_prompt_kernel(l_ref, q_ref, k_ref, v_ref, g_ref, la_ref, ng_ref, y_ref, s_ref, st_ref, *,
                       tg, n_pairs, scale):
    c_len = GLA_CHUNK
    half = LANES // 2
    ti = pl.program_id(1)

    @pl.when(ti == 0)
    def _():
        st_ref[...] = jnp.zeros_like(st_ref)

    lower = (_iota((c_len, c_len), 0) >= _iota((c_len, c_len), 1)).astype(BF16)
    lane = _iota((c_len, LANES), 1)
    first = lane < half
    r2 = _iota((2 * c_len, 2 * c_len), 0)
    c2 = _iota((2 * c_len, 2 * c_len), 1)
    sh = c_len.bit_length() - 1
    att_mask = jnp.logical_and((r2 >> sh) == (c2 >> sh), (c2 & (c_len - 1)) <= (r2 & (c_len - 1)))
    lane_sq = _iota((LANES, LANES), 1) < half
    ng = ng_ref[...]

    for c in range(tg // c_len):
        rows = slice(c * c_len, (c + 1) * c_len)
        la = la_ref[rows, :]
        b = _dot_split(la, lower, 3, left=True)
        b_last = b[c_len - 1:c_len, :]
        qd = q_ref[rows, :] * scale * jnp.exp(b)
        kk = k_ref[rows, :]
        kd = kk * jnp.exp(-b)
        ku = kk * jnp.exp(b_last - b)
        e_last = jnp.exp(b_last)
        for pr in range(n_pairs):
            ls = slice(pr * LANES, (pr + 1) * LANES)
            qd_p = qd[:, ls]
            qs = jnp.concatenate([jnp.where(first, qd_p, 0.0), jnp.where(first, 0.0, qd_p)],
                                 axis=0).astype(BF16)
            kd_p = kd[:, ls].astype(BF16)
            kd2 = jnp.concatenate([kd_p, kd_p], axis=0)
            att = jnp.where(att_mask, _dot_nt(qs, kd2), 0.0).astype(BF16)
            v_pair = v_ref[rows, 2 * pr * LANES:(2 * pr + 2) * LANES].astype(BF16)
            v_stack = jnp.concatenate([v_pair[:, :LANES], v_pair[:, LANES:]], axis=0)
            st = st_ref[pr]
            o = _dot(att, v_stack) + _dot_nt(qs, st.astype(BF16))
            upd = _dot_tn(v_pair, ku[:, ls].astype(BF16))
            st_ref[pr] = st * e_last[:, ls] + jnp.where(lane_sq, upd[:LANES], upd[LANES:])
            for s in range(2):
                hh = 2 * pr + s
                cs = slice(hh * LANES, (hh + 1) * LANES)
                oh = _rms(o[s * c_len:(s + 1) * c_len], ng)
                y_ref[rows, cs] = (oh * _silu(g_ref[rows, cs])).astype(y_ref.dtype)

    @pl.when(ti == pl.num_programs(1) - 1)
    def _():
        for pr in range(n_pairs):
            s_t = st_ref[pr].T
            s_ref[2 * pr] = s_t[:half]
            s_ref[2 * pr + 1] = s_t[half:]


def _gla_prompt(l_arr, p, la, ng, *, batch, n_heads, dk, dv, tg, cq, ck, cv, cg):
    m = p.shape[0]
    t = m // batch
    nt = t // tg
    wk = n_heads * dk
    wv = n_heads * dv
    n_pairs = n_heads // 2
    return pl.pallas_call(
        functools.partial(_gla_prompt_kernel, tg=tg, n_pairs=n_pairs, scale=dk ** -0.5),
        grid_spec=pltpu.PrefetchScalarGridSpec(
            num_scalar_prefetch=1,
            grid=(batch, nt),
            in_specs=[
                pl.BlockSpec((tg, wk), lambda b, i, l: (b * nt + i, cq)),
                pl.BlockSpec((tg, wk), lambda b, i, l: (b * nt + i, ck)),
                pl.BlockSpec((tg, wv), lambda b, i, l: (b * nt + i, cv)),
                pl.BlockSpec((tg, wv), lambda b, i, l: (b * nt + i, cg)),
                pl.BlockSpec((tg, wk), lambda b, i, l: (b * nt + i, 0)),
                pl.BlockSpec((None, 1, dv), lambda b, i, l: (l[0], 0, 0)),
            ],
            out_specs=[
                pl.BlockSpec((tg, wv), lambda b, i, l: (b * nt + i, 0)),
                pl.BlockSpec((None, n_heads, dk, dv), lambda b, i, l: (b, 0, 0, 0)),
            ],
            scratch_shapes=[pltpu.VMEM((n_pairs, LANES, LANES), F32)],
        ),
        out_shape=[jax.ShapeDtypeStruct((m, wv), BF16),
                   jax.ShapeDtypeStruct((batch, n_heads, dk, dv), F32)],
        compiler_params=pltpu.CompilerParams(dimension_semantics=("arbitrary", "arbitrary")),
        name="gla_prompt",
    )(l_arr, p, p, p, p, la, ng)


def _out_kernel(l_ref, x_ref, ya_ref, yb_ref, yc_ref, w0, w1, w2, w3, gain_ref, o_ref, *h_ref,
                final):
    wq = w0.shape[0]
    yb = yb_ref[...].astype(BF16)
    acc = x_ref[...] + _dot(ya_ref[...].astype(BF16), w0[...])
    acc = acc + _dot(yb[:, :wq], w1[...]) + _dot(yb[:, wq:], w2[...])
    acc = acc + _dot(yc_ref[...].astype(BF16), w3[...])
    if final:
        o_ref[...] = _rms(acc, gain_ref[...])
    else:
        o_ref[...] = acc
        h_ref[0][...] = _rms(acc, gain_ref[...]).astype(BF16)


def _out_proj(l_arr, x, ya, yb, yc, w_out, gain, *, tm, final):
    m, d = x.shape
    wq = ya.shape[1]
    assert yb.shape[1] == 2 * wq and yc.shape[1] == wq and w_out.shape[1] == 4 * wq
    wspec = lambda r: pl.BlockSpec((None, wq, d), lambda i, l, r=r: (l[0], r, 0))
    row_spec = pl.BlockSpec((tm, d), lambda i, l: (i, 0))
    out_specs = row_spec if final else [row_spec, row_spec]
    out_shape = jax.ShapeDtypeStruct((m, d), F32)
    if not final:
        out_shape = [out_shape, jax.ShapeDtypeStruct((m, d), BF16)]
    return pl.pallas_call(
        functools.partial(_out_kernel, final=final),
        grid_spec=pltpu.PrefetchScalarGridSpec(
            num_scalar_prefetch=1,
            grid=(m // tm,),
            in_specs=[
                pl.BlockSpec((tm, d), lambda i, l: (i, 0)),
                pl.BlockSpec((tm, wq), lambda i, l: (i, 0)),
                pl.BlockSpec((tm, 2 * wq), lambda i, l: (i, 0)),
                pl.BlockSpec((tm, wq), lambda i, l: (i, 0)),
                wspec(0), wspec(1), wspec(2), wspec(3),
                pl.BlockSpec((1, d), lambda i, l: (0, 0)),
            ],
            out_specs=out_specs,
        ),
        out_shape=out_shape,
        compiler_params=pltpu.CompilerParams(dimension_semantics=("arbitrary",)),
        name="out_proj",
    )(l_arr, x, ya, yb, yc, w_out, w_out, w_out, w_out, gain)


def _rev_excl_cumsum(x, later, parts):
    r, n = x.shape
    nb = n // LANES
    xs = jnp.concatenate([x[:, i * LANES:(i + 1) * LANES] for i in range(nb)], axis=0)
    cs = _dot_split(xs, later, parts)
    tot = jnp.sum(xs, axis=1, keepdims=True)
    run = jnp.zeros((r, 1), F32)
    blocks = [None] * nb
    for i in range(nb - 1, -1, -1):
        blocks[i] = cs[i * r:(i + 1) * r] + run
        run = run + tot[i * r:(i + 1) * r]
    return jnp.concatenate(blocks, axis=1), run


def _head_rows(row_vec, h):
    part = jnp.broadcast_to(row_vec[:, h * LANES:(h + 1) * LANES], (SUBLANES, LANES))
    return jnp.where(_iota((SUBLANES, LANES), 0) == h, part, 0.0)


def _head_block(buf, slot, h, tok, n_heads):
    return buf[slot, pl.ds(h, tok, stride=n_heads), :].astype(BF16)


def _paged_scores(q_row, kbuf, slot, tok, n_heads):
    z = None
    for h in range(n_heads):
        t = _dot_nt(_head_rows(q_row, h).astype(BF16), _head_block(kbuf, slot, h, tok, n_heads))
        z = t if z is None else z + t
    return z


def _own_rows(acc_ref, n_heads, denom=None):
    parts = []
    for h in range(n_heads):
        a = acc_ref[h] if denom is None else acc_ref[h] / denom
        parts.append(a[h:h + 1, :])
    return jnp.concatenate(parts, axis=1)


def _page_copies(pt_ref, l, bb, gg, slot, srcs, bufs, sem, *, n_groups, g_pages):
    base = (n_groups - 1 - gg) * g_pages
    out = []
    for r in range(g_pages):
        pg = pt_ref[bb, base + r]
        for si, (src, buf) in enumerate(zip(srcs, bufs)):
            rows = src.shape[2]
            out.append(pltpu.make_async_copy(
                src.at[l, pg], buf.at[slot, pl.ds(r * rows, rows)], sem.at[si, slot]))
    return out


def _paged_pipeline(pt_ref, l, srcs, bufs, sem, *, n_groups, g_pages):
    bb = pl.program_id(0)
    gg = pl.program_id(1)
    step = bb * n_groups + gg
    total = pl.num_programs(0) * n_groups
    slot = lax.rem(step, 2)
    mk = functools.partial(_page_copies, pt_ref, l, srcs=srcs, bufs=bufs, sem=sem,
                           n_groups=n_groups, g_pages=g_pages)

    @pl.when(step == 0)
    def _():
        for cp in mk(bb, gg, slot):
            cp.start()

    @pl.when(step + 1 < total)
    def _():
        wrap = gg + 1 == n_groups
        nb = jnp.where(wrap, bb + 1, bb)
        ng = jnp.where(wrap, 0, gg + 1)
        for cp in mk(nb, ng, 1 - slot):
            cp.start()

    for cp in mk(bb, gg, slot):
        cp.wait()
    return slot


def _sb_decode_kernel(pt_ref, l_ref, p_ref, kc_ref, vc_ref, o_ref, kbuf, vbuf, sem, acc_ref,
                      carry_ref, flag_ref, *, n_groups, g_pages, tok, n_heads, scale, cq, cg):
    bb = pl.program_id(0)
    gg = pl.program_id(1)
    w = n_heads * LANES
    step = bb * n_groups + gg
    slot = lax.rem(step, 2)
    mk = functools.partial(_page_copies, pt_ref, l_ref[0], srcs=(kc_ref, vc_ref),
                           bufs=(kbuf, vbuf), sem=sem, n_groups=n_groups, g_pages=g_pages)

    @pl.when(step == 0)
    def _():
        for cp in mk(bb, gg, slot):
            cp.start()
        flag_ref[1] = 1

    @pl.when(gg == 0)
    def _():
        acc_ref[...] = jnp.zeros_like(acc_ref)
        carry_ref[...] = jnp.zeros_like(carry_ref)
        flag_ref[0] = 1

    live = flag_ref[0] > 0
    requested = flag_ref[1] > 0
    last_group = gg + 1 == n_groups
    want_next = jnp.where(last_group, bb + 1 < pl.num_programs(0), live)

    @pl.when(want_next)
    def _():
        nb = jnp.where(last_group, bb + 1, bb)
        ng = jnp.where(last_group, 0, gg + 1)
        for cp in mk(nb, ng, 1 - slot):
            cp.start()

    @pl.when(requested)
    def _():
        for cp in mk(bb, gg, slot):
            cp.wait()

    @pl.when(jnp.logical_and(requested, live))
    def _():
        q_row = p_ref[:, cq * LANES:cq * LANES + w] * scale
        later = (_iota((LANES, LANES), 0) > _iota((LANES, LANES), 1)).astype(BF16)
        z = _paged_scores(q_row, kbuf, slot, tok, n_heads)
        l1m = -_softplus(z)
        cum, tot = _rev_excl_cumsum(l1m, later, 2)
        wgt = jnp.exp(z + l1m + cum + carry_ref[...]).astype(BF16)
        for h in range(n_heads):
            acc_ref[h] += _dot(wgt, _head_block(vbuf, slot, h, tok, n_heads))
        carry = carry_ref[...] + tot
        carry_ref[...] = carry
        heads = _iota(carry.shape, 0) < n_heads
        flag_ref[0] = (jnp.max(jnp.where(heads, carry, NEG_BIG)) > SB_DEAD).astype(jnp.int32)

    flag_ref[1] = want_next.astype(jnp.int32)

    @pl.when(gg == n_groups - 1)
    def _():
        gate = p_ref[:, cg * LANES:cg * LANES + w]
        o_ref[...] = _own_rows(acc_ref, n_heads) * _silu(gate)


def _sb_decode(page_table, l_arr, p, kc, vc, *, n_heads, g_pages, cq, cg):
    n_seq, n_pages = page_table.shape
    rows = kc.shape[2]
    w = n_heads * LANES
    n_groups = n_pages // g_pages
    tok = g_pages * rows // n_heads
    return pl.pallas_call(
        functools.partial(_sb_decode_kernel, n_groups=n_groups, g_pages=g_pages, tok=tok,
                          n_heads=n_heads, scale=LANES ** -0.5, cq=cq, cg=cg),
        grid_spec=pltpu.PrefetchScalarGridSpec(
            num_scalar_prefetch=2,
            grid=(n_seq, n_groups),
            in_specs=[
                pl.BlockSpec((None, 1, p.shape[2]), lambda b, g, pt, l: (b, 0, 0)),
                pl.BlockSpec(memory_space=pl.ANY),
                pl.BlockSpec(memory_space=pl.ANY),
            ],
            out_specs=pl.BlockSpec((None, 1, w), lambda b, g, pt, l: (b, 0, 0)),
            scratch_shapes=[
                pltpu.VMEM((2, g_pages * rows, LANES), F32),
                pltpu.VMEM((2, g_pages * rows, LANES), F32),
                pltpu.SemaphoreType.DMA((2, 2)),
                pltpu.VMEM((n_heads, SUBLANES, LANES), F32),
                pltpu.VMEM((SUBLANES, 1), F32),
                pltpu.SMEM((2,), jnp.int32),
            ],
        ),
        out_shape=jax.ShapeDtypeStruct((n_seq, 1, w), F32),
        compiler_params=pltpu.CompilerParams(dimension_semantics=("arbitrary", "arbitrary")),
        name="sb_decode",
    )(page_table, l_arr, p, kc, vc)


def _fox_decode_kernel(pt_ref, l_ref, p_ref, lfn_ref, kc_ref, vc_ref, lc_ref, o_ref, kbuf, vbuf,
                       lbuf, sem, acc_ref, m_ref, l_sum_ref, carry_ref, *, n_groups, g_pages, tok,
                       n_heads, scale, cq, ck, cv, cg):
    gg = pl.program_id(1)
    w = n_heads * LANES
    slot = _paged_pipeline(pt_ref, l_ref[0], (kc_ref, vc_ref, lc_ref), (kbuf, vbuf, lbuf), sem,
                           n_groups=n_groups, g_pages=g_pages)
    q_row = p_ref[:, cq * LANES:cq * LANES + w] * scale

    @pl.when(gg == 0)
    def _():
        k_new = p_ref[:, ck * LANES:ck * LANES + w]
        v_new = p_ref[:, cv * LANES:cv * LANES + w]
        z_new = jnp.zeros((SUBLANES, 1), F32)
        for h in range(n_heads):
            cs = slice(h * LANES, (h + 1) * LANES)
            z_new = z_new + jnp.sum(_head_rows(q_row, h) * k_new[:, cs], axis=1, keepdims=True)
            acc_ref[h] = jnp.broadcast_to(v_new[:, cs], (SUBLANES, LANES))
        m_ref[...] = z_new
        l_sum_ref[...] = jnp.ones_like(l_sum_ref)
        pick = _iota((SUBLANES, LANES), 1) == _iota((SUBLANES, LANES), 0) + SIDE_FC
        lf_row = jnp.broadcast_to(lfn_ref[...], (SUBLANES, LANES))
        carry_ref[...] = jnp.sum(jnp.where(pick, lf_row, 0.0), axis=1, keepdims=True)

    later = (_iota((LANES, LANES), 0) > _iota((LANES, LANES), 1)).astype(BF16)
    lf = jnp.concatenate([lbuf[slot, pl.ds(r * SUBLANES, SUBLANES), :] for r in range(g_pages)],
                         axis=1)
    bias, tot = _rev_excl_cumsum(lf, later, 3)
    z = _paged_scores(q_row, kbuf, slot, tok, n_heads) + (bias + carry_ref[...])
    m_old = m_ref[...]
    m_new = jnp.maximum(m_old, jnp.max(z, axis=1, keepdims=True))
    alpha = jnp.exp(m_old - m_new)
    pr = jnp.exp(z - m_new)
    l_sum_ref[...] = alpha * l_sum_ref[...] + jnp.sum(pr, axis=1, keepdims=True)
    pr = pr.astype(BF16)
    for h in range(n_heads):
        acc_ref[h] = alpha * acc_ref[h] + _dot(pr, _head_block(vbuf, slot, h, tok, n_heads))
    m_ref[...] = m_new
    carry_ref[...] += tot

    @pl.when(gg == n_groups - 1)
    def _():
        gate = p_ref[:, cg * LANES:cg * LANES + w]
        o_ref[...] = _own_rows(acc_ref, n_heads, l_sum_ref[...]) * _silu(gate)


def _fox_decode(page_table, l_arr, p, lf_new, kc, vc, lc, *, n_heads, g_pages, cq, ck, cv, cg):
    n_seq, n_pages = page_table.shape
    rows = kc.shape[2]
    w = n_heads * LANES
    n_groups = n_pages // g_pages
    tok = g_pages * rows // n_heads
    return pl.pallas_call(
        functools.partial(_fox_decode_kernel, n_groups=n_groups, g_pages=g_pages, tok=tok,
                          n_heads=n_heads, scale=LANES ** -0.5, cq=cq, ck=ck, cv=cv, cg=cg),
        grid_spec=pltpu.PrefetchScalarGridSpec(
            num_scalar_prefetch=2,
            grid=(n_seq, n_groups),
            in_specs=[
                pl.BlockSpec((None, 1, p.shape[2]), lambda b, g, pt, l: (b, 0, 0)),
                pl.BlockSpec((None, 1, lf_new.shape[2]), lambda b, g, pt, l: (b, 0, 0)),
                pl.BlockSpec(memory_space=pl.ANY),
                pl.BlockSpec(memory_space=pl.ANY),
                pl.BlockSpec(memory_space=pl.ANY),
            ],
            out_specs=pl.BlockSpec((None, 1, w), lambda b, g, pt, l: (b, 0, 0)),
            scratch_shapes=[
                pltpu.VMEM((2, g_pages * rows, LANES), F32),
                pltpu.VMEM((2, g_pages * rows, LANES), F32),
                pltpu.VMEM((2, g_pages * SUBLANES, LANES), F32),
                pltpu.SemaphoreType.DMA((3, 2)),
                pltpu.VMEM((n_heads, SUBLANES, LANES), F32),
                pltpu.VMEM((SUBLANES, 1), F32),
                pltpu.VMEM((SUBLANES, 1), F32),
                pltpu.VMEM((SUBLANES, 1), F32),
            ],
        ),
        out_shape=jax.ShapeDtypeStruct((n_seq, 1, w), F32),
        compiler_params=pltpu.CompilerParams(dimension_semantics=("arbitrary", "arbitrary")),
        name="fox_decode",
    )(page_table, l_arr, p, lf_new, kc, vc, lc)


def _gla_decode_kernel(l_ref, p_ref, la_ref, s0_ref, ng_ref, y_ref, s_ref, *, n_heads, dk, dv,
                       scale, cq, ck, cv, cg):
    bb = pl.program_id(0)
    wk = n_heads * dk

    q_row = p_ref[:,cq * LANES:cq * LANES + wk]
    k_row = p_ref[:,ck * LANES:ck * LANES + wk]
    e_row = jnp.exp(la_ref[...])
    lane = _iota((dk, wk), 1)
    sub = _iota((dk, wk), 0)
    ng = ng_ref[...]

    def column(row_vec, hh):
        pick = lane == sub + hh * dk
        return jnp.sum(jnp.where(pick, jnp.broadcast_to(row_vec, (dk, wk)), 0.0),
                       axis=1, keepdims=True)

    for hh in range(n_heads):
        cs = slice(hh * dv, (hh + 1) * dv)
        v_row = p_ref[:,cv * LANES + hh * dv:cv * LANES + (hh + 1) * dv]
        s_new = column(e_row, hh) * s0_ref[hh] + column(k_row, hh) * v_row
        s_ref[hh] = s_new
        o = jnp.sum((column(q_row, hh) * scale) * s_new, axis=0, keepdims=True)
        gate = p_ref[:,cg * LANES + hh * dv:cg * LANES + (hh + 1) * dv]
        y_ref[:, cs] = _rms(o, ng) * _silu(gate)


def _gla_decode(l_arr, p, la, state, ng, *, cq, ck, cv, cg):
    _, n_seq, n_heads, dk, dv = state.shape
    return pl.pallas_call(
        functools.partial(_gla_decode_kernel, n_heads=n_heads, dk=dk, dv=dv, scale=dk ** -0.5,
                          cq=cq, ck=ck, cv=cv, cg=cg),
        grid_spec=pltpu.PrefetchScalarGridSpec(
            num_scalar_prefetch=1,
            grid=(n_seq,),
            in_specs=[
                pl.BlockSpec((None, 1, p.shape[2]), lambda b, l: (b, 0, 0)),
                pl.BlockSpec((None, 1, la.shape[2]), lambda b, l: (b, 0, 0)),
                pl.BlockSpec((None, None, n_heads, dk, dv), lambda b, l: (l[0], b, 0, 0, 0)),
                pl.BlockSpec((None, 1, dv), lambda b, l: (l[0], 0, 0)),
            ],
            out_specs=[
                pl.BlockSpec((None, 1, n_heads * dv), lambda b, l: (b, 0, 0)),
                pl.BlockSpec((None, n_heads, dk, dv), lambda b, l: (b, 0, 0, 0)),
            ],
        ),
        out_shape=[jax.ShapeDtypeStruct((n_seq, 1, n_heads * dv), F32),
                   jax.ShapeDtypeStruct((n_seq, n_heads, dk, dv), F32)],
        compiler_params=pltpu.CompilerParams(dimension_semantics=("arbitrary",)),
        name="gla_decode",
    )(l_arr, p, la, state, ng)


DEC_ROWS = 16
PROJ_TM = 1024
SIDE_TA = 256
ATT_TB = 256
GLA_TG = 512
OUT_TM = 512
PAGES_PER_STEP = 16
SB_PAGES_PER_STEP = 8


def kernel(x_prompt, x_sample, cache_sb_k, cache_sb_v, cache_fox_k, cache_fox_v, cache_fox_logf,
           state_gla, page_table, norm_g, w_in, w_gate_b, b_gate, b_forget, q_norm_g, k_norm_g,
           gla_norm_g, w_out, norm_f):
    batch, seq, d = x_prompt.shape
    n_seq = x_sample.shape[0]
    depth, n_pool, page, h_sb, hd = cache_sb_k.shape
    h_fox = cache_fox_k.shape[3]
    _, _, h_gla, dk, dv = state_gla.shape
    rank = w_gate_b.shape[1]
    assert hd == LANES and dv == LANES and 2 * dk == LANES and x_sample.shape[1] == 1
    w_sb, w_fox, w_gk, w_gv = h_sb * hd, h_fox * hd, h_gla * dk, h_gla * dv
    assert w_sb == w_fox == w_gk and w_gv == 2 * w_sb

    sizes = (w_sb,) * 4 + (w_gk, w_gk, w_gv, w_gv, rank) + (w_fox,) * 4 + (h_fox,)
    offs = [0]
    for s in sizes:
        offs.append(offs[-1] + s)
    assert offs[-1] == w_in.shape[2]
    o_ab, o_qc, o_fc = offs[8], offs[9], offs[13]
    n_main = o_ab + (o_fc - o_qc)
    c_qa, c_ka, c_va, c_ga = (offs[i] // LANES for i in range(4))
    c_qb, c_kb, c_vb, c_gb = (offs[i] // LANES for i in range(4, 8))
    c_qc, c_kc, c_vc, c_gc = ((offs[i] - rank) // LANES for i in range(9, 13))

    assert SIDE_AB + rank <= SIDE_FC and SIDE_FC + h_fox <= LANES
    w_t = jnp.swapaxes(w_in, 1, 2).astype(BF16)
    w_b = w_t[:, o_qc:o_fc]
    w_side = jnp.zeros((depth, LANES, d), BF16)
    w_side = w_side.at[:, SIDE_AB:SIDE_AB + rank].set(w_t[:, o_ab:o_qc])
    w_side = w_side.at[:, SIDE_FC:SIDE_FC + h_fox].set(w_t[:, o_fc:])
    side = _side
    wgb = jnp.zeros((depth, LANES, w_gk), F32).at[:, SIDE_AB:SIDE_AB + rank].set(w_gate_b)
    wgb = wgb.astype(BF16)
    bg = b_gate.reshape(depth, 1, w_gk)
    bfg = jnp.zeros((depth, 1, LANES), F32).at[:, 0, SIDE_FC:SIDE_FC + h_fox].set(b_forget)
    w_o = w_out.astype(BF16)
    ng3 = norm_g.reshape(depth, 1, d)
    qg3 = q_norm_g.reshape(depth, 1, hd)
    kg3 = k_norm_g.reshape(depth, 1, hd)
    gg3 = gla_norm_g.reshape(depth, 1, dv)
    nf2 = norm_f.reshape(1, d)

    kc_sb = cache_sb_k.reshape(depth, n_pool, page * h_sb, hd)
    vc_sb = cache_sb_v.reshape(depth, n_pool, page * h_sb, hd)
    kc_fx = cache_fox_k.reshape(depth, n_pool, page * h_fox, hd)
    vc_fx = cache_fox_v.reshape(depth, n_pool, page * h_fox, hd)
    lc_fx = jnp.pad(jnp.swapaxes(cache_fox_logf, 2, 3), ((0, 0), (0, 0), (0, SUBLANES - h_fox), (0, 0)))

    m_p = batch * seq
    hp = x_prompt.reshape(m_p, d)
    hs = jnp.pad(x_sample.reshape(n_seq, d), ((0, DEC_ROWS - n_seq), (0, 0)))

    tn = 2 * w_fox
    assert o_ab % tn == 0 and (o_fc - o_qc) % tn == 0
    proj = functools.partial(_proj, tn=tn, ja=o_ab // tn)

    l0 = jnp.zeros((1,), jnp.int32)
    hp_n = _norm(l0, hp, ng3, tm=min(OUT_TM, m_p))
    hs_n = _norm(l0, hs, ng3, tm=DEC_ROWS)
    outs = [[] for _ in range(12)]
    ka_all = jnp.zeros((depth, batch, seq * h_sb, hd), F32)
    va_all = jnp.zeros((depth, batch, seq * h_sb, hd), F32)
    kc_all = jnp.zeros((depth, batch, seq * h_fox, hd), F32)
    vc_all = jnp.zeros((depth, batch, seq * h_fox, hd), F32)
    for layer in range(depth):
        l_arr = jnp.full((1,), layer, jnp.int32)
        final = layer == depth - 1
        gain = nf2 if final else ng3[layer + 1]

        p = proj(l_arr, hp_n, w_t, w_b, qg3, kg3, tm=min(PROJ_TM, m_p))
        la, lf, fq, ft = side(l_arr, hp_n, w_side, wgb, bg, bfg, batch=batch,
                              ta=min(SIDE_TA, seq), n_heads=h_fox, with_cumsum=True)
        tb = min(ATT_TB, seq)
        ya, ka_all, va_all = _sb_prompt(l_arr, p, ka_all, va_all, batch=batch, n_heads=h_sb, tb=tb,
                                        cq=c_qa, ck=c_ka, cv=c_va, cg=c_ga)
        yb, s_fin = _gla_prompt(l_arr, p, la, gg3, batch=batch, n_heads=h_gla, dk=dk, dv=dv,
                                tg=min(GLA_TG, seq), cq=c_qb * LANES // w_gk, ck=c_kb * LANES // w_gk,
                                cv=c_vb * LANES // w_gv, cg=c_gb * LANES // w_gv)
        yc, kc_all, vc_all = _fox_prompt(l_arr, p, fq, ft, kc_all, vc_all, batch=batch,
                                         n_heads=h_fox, tb=tb, cq=c_qc, ck=c_kc, cv=c_vc, cg=c_gc)
        hp = _out_proj(l_arr, hp, ya, yb, yc, w_o, gain, tm=min(OUT_TM, m_p), final=final)
        if not final:
            hp, hp_n = hp
        outs[4].append(lf[:, SIDE_FC:SIDE_FC + h_fox].reshape(batch, seq, h_fox))
        outs[5].append(s_fin)

        ps = proj(l_arr, hs_n, w_t, w_b, qg3, kg3, tm=DEC_ROWS)
        las, lfs = side(l_arr, hs_n, w_side, wgb, bg, bfg, batch=1, ta=DEC_ROWS,
                        n_heads=h_fox, with_cumsum=False)
        ps3 = ps.reshape(DEC_ROWS, 1, -1)
        g_pages = min(PAGES_PER_STEP, page_table.shape[1])
        yas = _sb_decode(page_table, l_arr, ps3, kc_sb, vc_sb, n_heads=h_sb,
                         g_pages=min(SB_PAGES_PER_STEP, page_table.shape[1]), cq=c_qa, cg=c_ga)
        ybs, s_new = _gla_decode(l_arr, ps3, las.reshape(DEC_ROWS, 1, -1), state_gla, gg3,
                                 cq=c_qb, ck=c_kb, cv=c_vb, cg=c_gb)
        ycs = _fox_decode(page_table, l_arr, ps3, lfs.reshape(DEC_ROWS, 1, -1), kc_fx, vc_fx, lc_fx,
                          n_heads=h_fox, g_pages=g_pages, cq=c_qc, ck=c_kc, cv=c_vc, cg=c_gc)
        pad_rows = lambda y: jnp.pad(y.reshape(n_seq, -1), ((0, DEC_ROWS - n_seq), (0, 0)))
        hs = _out_proj(l_arr, hs, pad_rows(yas), pad_rows(ybs), pad_rows(ycs), w_o, gain,
                       tm=DEC_ROWS, final=final)
        if not final:
            hs, hs_n = hs
        cols = lambda c, w: ps[:n_seq, c * LANES:c * LANES + w]
        outs[6].append(cols(c_ka, w_sb).reshape(n_seq, 1, h_sb, hd))
        outs[7].append(cols(c_va, w_sb).reshape(n_seq, 1, h_sb, hd))
        outs[8].append(cols(c_kc, w_fox).reshape(n_seq, 1, h_fox, hd))
        outs[9].append(cols(c_vc, w_fox).reshape(n_seq, 1, h_fox, hd))
        outs[10].append(lfs[:n_seq, SIDE_FC:SIDE_FC + h_fox].reshape(n_seq, 1, h_fox))
        outs[11].append(s_new)

    y_prompt = hp.reshape(batch, seq, d)
    y_sample = hs[:n_seq].reshape(n_seq, 1, d)
    prompt_kv = (ka_all.reshape(depth, batch, seq, h_sb, hd), va_all.reshape(depth, batch, seq, h_sb, hd),
                 kc_all.reshape(depth, batch, seq, h_fox, hd), vc_all.reshape(depth, batch, seq, h_fox, hd))
    return (y_prompt, y_sample) + prompt_kv + tuple(jnp.stack(o) for o in outs[4:])
```
